```python
import math
import jax, jax.numpy as jnp
from jax import lax
import numpy as np

D_MODEL = 1024
BATCH = 8
SEQ = 8192
DEPTH = 1

SSD_EXPAND = 2
SSD_D_INNER = SSD_EXPAND * D_MODEL
SSD_HEAD_DIM = 64
SSD_N_HEADS = SSD_D_INNER // SSD_HEAD_DIM
SSD_N_GROUPS = 4
SSD_HEADS_PER_GROUP = SSD_N_HEADS // SSD_N_GROUPS
SSD_D_STATE = 128
SSD_CONV_WIDTH = 4
SSD_CONV_DIM = SSD_D_INNER + 2 * SSD_N_GROUPS * SSD_D_STATE
DT_MIN = 0.001
DT_MAX = 0.1
A_MIN = 1.0
A_MAX = 16.0
RET_N_HEADS = 8
RET_QK_HEAD_DIM = D_MODEL // RET_N_HEADS
RET_V_HEAD_DIM = 2 * RET_QK_HEAD_DIM
RET_QK_DIM = RET_N_HEADS * RET_QK_HEAD_DIM
RET_V_DIM = RET_N_HEADS * RET_V_HEAD_DIM
ROPE_BASE = 10000.0
MAX_POS_OFFSET = 4096
CHUNK = 128
FFN_DIM = 2816
FFN_RESIDUAL_WEIGHT = 0.5
N_MOD = 9
EPS = 1e-6
MIX_IN_SIZES = (SSD_D_INNER, SSD_CONV_DIM, SSD_N_HEADS, RET_QK_DIM, RET_QK_DIM,
                RET_V_DIM, RET_V_DIM, D_MODEL, D_MODEL)
MIX_IN_DIM = sum(MIX_IN_SIZES)

kernel_name = "hybrid_ssd_retention_macaron_adaln"


def rms_normalize(x):
    xf = x.astype(jnp.float32)
    return (xf * lax.rsqrt(jnp.mean(xf * xf, axis=-1, keepdims=True) + EPS)).astype(x.dtype)


def rms_norm(x, w):
    return rms_normalize(x) * w


def modulate(x, shift, scale):
    return x * (1.0 + scale[:, None, :]) + shift[:, None, :]


def swiglu(x, w_in, w_out):
    gate, up = jnp.split(x @ w_in, 2, axis=-1)
    return (jax.nn.silu(gate) * up) @ w_out


def rotary(x, positions):
    half = x.shape[-1] // 2
    inv_freq = ROPE_BASE ** (-jnp.arange(half, dtype=jnp.float32) / half)
    ang = positions.astype(jnp.float32)[..., None] * inv_freq
    cos = jnp.cos(ang)[:, :, None, :]
    sin = jnp.sin(ang)[:, :, None, :]
    xf = x.astype(jnp.float32)
    x1, x2 = xf[..., :half], xf[..., half:]
    return jnp.concatenate([x1 * cos - x2 * sin, x2 * cos + x1 * sin], axis=-1).astype(x.dtype)


def causal_depthwise_conv(x, w, b):
    k, ch = w.shape
    y = lax.conv_general_dilated(x, w[:, None, :].astype(x.dtype), window_strides=(1,),
                                 padding=[(k - 1, 0)], dimension_numbers=('NWC', 'WIO', 'NWC'),
                                 feature_group_count=ch)
    return y + b


def segsum_exp(a):
    cs = jnp.cumsum(a, axis=-1)
    diff = cs[..., :, None] - cs[..., None, :]
    n = a.shape[-1]
    mask = jnp.tril(jnp.ones((n, n), dtype=bool))
    return jnp.exp(jnp.where(mask, diff, -jnp.inf))


def exclusive_chunk_scan(states, decays):
    def step(carry, inp):
        s, d = inp
        return carry * d[..., None, None] + s, carry
    init = jnp.zeros(states.shape[1:], jnp.float32)
    _, prev = lax.scan(step, init, (states.astype(jnp.float32), decays.astype(jnp.float32)))
    return prev


def ssd_branch(z, xbc, dt_raw, conv_w, conv_b, dt_bias, a_log, d_skip, norm_w):
    bsz, s, _ = z.shape
    nc = s // CHUNK
    g_, r_, p_, n_ = SSD_N_GROUPS, SSD_HEADS_PER_GROUP, SSD_HEAD_DIM, SSD_D_STATE
    xbc = jax.nn.silu(causal_depthwise_conv(xbc, conv_w, conv_b))
    xs, bm, cm = jnp.split(xbc, [SSD_D_INNER, SSD_D_INNER + g_ * n_], axis=-1)
    xs = xs.reshape(bsz, nc, CHUNK, g_, r_, p_)
    bm = bm.reshape(bsz, nc, CHUNK, g_, n_)
    cm = cm.reshape(bsz, nc, CHUNK, g_, n_)
    dt = jax.nn.softplus(dt_raw.astype(jnp.float32) + dt_bias.astype(jnp.float32))
    dt = dt.reshape(bsz, nc, CHUNK, g_, r_)
    a_head = -jnp.exp(a_log.astype(jnp.float32)).reshape(g_, r_)
    a = jnp.transpose(dt * a_head, (0, 3, 4, 1, 2))
    a_cs = jnp.cumsum(a, axis=-1)
    xdt = xs * dt[..., None]
    cb = jnp.einsum('bclgn,bcsgn->bgcls', cm, bm)
    scores = cb[:, :, None] * segsum_exp(a)
    y_diag = jnp.einsum('bgrcls,bcsgrp->bclgrp', scores, xdt)
    decay_to_end = jnp.exp(a_cs[..., -1:] - a_cs)
    chunk_states = jnp.einsum('bclgn,bgrcl,bclgrp->bcgrpn', bm, decay_to_end, xdt)
    chunk_decay = jnp.exp(a_cs[..., -1])
    prev_states = exclusive_chunk_scan(jnp.moveaxis(chunk_states, 1, 0),
                                       jnp.moveaxis(chunk_decay, -1, 0))
    y_off = jnp.einsum('bclgn,cbgrpn,bgrcl->bclgrp', cm, prev_states, jnp.exp(a_cs))
    y = y_diag + y_off + xs * d_skip.reshape(g_, r_)[..., None]
    y = y.reshape(bsz, s, SSD_D_INNER).astype(z.dtype)
    yg = (y * jax.nn.silu(z)).reshape(bsz, s, g_, SSD_D_INNER // g_)
    return rms_normalize(yg).reshape(bsz, s, SSD_D_INNER) * norm_w


def retention_branch(q, k, v, g, positions, norm_w):
    bsz, s, _ = q.shape
    nc = s // CHUNK
    h_, dk, dv = RET_N_HEADS, RET_QK_HEAD_DIM, RET_V_HEAD_DIM
    q = rotary(q.reshape(bsz, s, h_, dk), positions)
    k = rotary(k.reshape(bsz, s, h_, dk), positions) * (dk ** -0.5)
    qc = q.reshape(bsz, nc, CHUNK, h_, dk)
    kc = k.reshape(bsz, nc, CHUNK, h_, dk)
    vc = v.reshape(bsz, nc, CHUNK, h_, dv)
    log_gamma = jnp.log1p(-jnp.exp2(-5.0 - jnp.arange(h_, dtype=jnp.float32)))
    idx = jnp.arange(CHUNK, dtype=jnp.float32)
    rel = idx[:, None] - idx[None, :]
    decay_mask = jnp.where(rel >= 0, jnp.exp(log_gamma[:, None, None] * jnp.maximum(rel, 0.0)), 0.0)
    scores = jnp.einsum('bclhd,bcshd->bchls', qc, kc) * decay_mask
    y_intra = jnp.einsum('bchls,bcshe->bclhe', scores, vc)
    k_decay = jnp.exp(log_gamma[None, :] * (CHUNK - 1.0 - idx)[:, None])
    chunk_states = jnp.einsum('bcshd,sh,bcshe->bchde', kc, k_decay, vc)
    chunk_decay = jnp.broadcast_to(jnp.exp(log_gamma * CHUNK), (nc, bsz, h_))
    prev_states = exclusive_chunk_scan(jnp.moveaxis(chunk_states, 1, 0), chunk_decay)
    q_decay = jnp.exp(log_gamma[None, :] * (idx + 1.0)[:, None])
    y_inter = jnp.einsum('bclhd,cbhde,lh->bclhe', qc, prev_states, q_decay)
    y = (y_intra + y_inter).reshape(bsz, s, h_, dv).astype(v.dtype)
    y = rms_normalize(y).reshape(bsz, s, RET_V_DIM) * norm_w
    return y * jax.nn.silu(g)


def hybrid_mixer(u, positions, w_in, gate_b, ssd_conv_w, ssd_conv_b, ssd_dt_bias, ssd_a_log,
                 ssd_d, ssd_norm_w, ret_norm_w, w_br_ssd, w_br_ret, w_out):
    proj = u @ w_in
    splits = np.cumsum(MIX_IN_SIZES)[:-1].tolist()
    z, xbc, dt_raw, q, k, v, g, gate_ssd_raw, gate_ret_raw = jnp.split(proj, splits, axis=-1)
    gate_b_ssd, gate_b_ret = jnp.split(gate_b, 2, axis=-1)
    y_ssd = ssd_branch(z, xbc, dt_raw, ssd_conv_w, ssd_conv_b, ssd_dt_bias, ssd_a_log,
                       ssd_d, ssd_norm_w) @ w_br_ssd
    y_ret = retention_branch(q, k, v, g, positions, ret_norm_w) @ w_br_ret
    gate_ssd = jax.nn.sigmoid(gate_ssd_raw + gate_b_ssd)
    gate_ret = jax.nn.sigmoid(gate_ret_raw + gate_b_ret)
    return (gate_ssd * y_ssd + gate_ret * y_ret) @ w_out


def setup_inputs(seed: int = 0) -> dict:
    key = jax.random.key(seed)
    ks = iter(jax.random.split(key, 40))
    f32 = jnp.float32
    L = DEPTH

    def dense(shape, fan_in, scale=1.0):
        return jax.random.normal(next(ks), shape, f32) * (scale * fan_in ** -0.5)

    def gain(shape):
        return 1.0 + 0.02 * jax.random.normal(next(ks), shape, f32)

    def bias(shape):
        return 0.01 * jax.random.normal(next(ks), shape, f32)

    x = jax.random.normal(next(ks), (BATCH, SEQ, D_MODEL), f32)
    c = jax.random.normal(next(ks), (BATCH, D_MODEL), f32)
    offset = jax.random.randint(next(ks), (BATCH, 1), 0, MAX_POS_OFFSET, dtype=jnp.int32)
    positions = (offset + jnp.arange(SEQ, dtype=jnp.int32)[None, :]).astype(jnp.int32)
    ada_w = dense((L, D_MODEL, N_MOD * D_MODEL), D_MODEL, 0.5)
    ada_b = bias((L, N_MOD * D_MODEL))
    norm_ffn1_w = gain((L, D_MODEL))
    ffn1_w_in = dense((L, D_MODEL, 2 * FFN_DIM), D_MODEL)
    ffn1_w_out = dense((L, FFN_DIM, D_MODEL), FFN_DIM)
    norm_mix_w = gain((L, D_MODEL))
    mix_w_in = dense((L, D_MODEL, MIX_IN_DIM), D_MODEL)
    mix_gate_b = bias((L, 2 * D_MODEL))
    ssd_conv_w = dense((L, SSD_CONV_WIDTH, SSD_CONV_DIM), SSD_CONV_WIDTH)
    ssd_conv_b = bias((L, SSD_CONV_DIM))
    dt = jnp.exp(jax.random.uniform(next(ks), (L, SSD_N_HEADS), f32, math.log(DT_MIN), math.log(DT_MAX)))
    ssd_dt_bias = dt + jnp.log(-jnp.expm1(-dt))
    ssd_a_log = jnp.log(jax.random.uniform(next(ks), (L, SSD_N_HEADS), f32, A_MIN, A_MAX))
    ssd_d = gain((L, SSD_N_HEADS))
    ssd_norm_w = gain((L, SSD_D_INNER))
    ret_norm_w = gain((L, RET_V_DIM))
    w_br_ssd = dense((L, SSD_D_INNER, D_MODEL), SSD_D_INNER)
    w_br_ret = dense((L, RET_V_DIM, D_MODEL), RET_V_DIM)
    mix_w_out = dense((L, D_MODEL, D_MODEL), D_MODEL)
    norm_ffn2_w = gain((L, D_MODEL))
    ffn2_w_in = dense((L, D_MODEL, 2 * FFN_DIM), D_MODEL)
    ffn2_w_out = dense((L, FFN_DIM, D_MODEL), FFN_DIM)
    norm_final_w = gain((D_MODEL,))
    return {"x": x, "c": c, "positions": positions, "ada_w": ada_w, "ada_b": ada_b,
            "norm_ffn1_w": norm_ffn1_w, "ffn1_w_in": ffn1_w_in, "ffn1_w_out": ffn1_w_out,
            "norm_mix_w": norm_mix_w, "mix_w_in": mix_w_in, "mix_gate_b": mix_gate_b,
            "ssd_conv_w": ssd_conv_w, "ssd_conv_b": ssd_conv_b, "ssd_dt_bias": ssd_dt_bias,
            "ssd_a_log": ssd_a_log, "ssd_d": ssd_d, "ssd_norm_w": ssd_norm_w,
            "ret_norm_w": ret_norm_w, "w_br_ssd": w_br_ssd, "w_br_ret": w_br_ret,
            "mix_w_out": mix_w_out, "norm_ffn2_w": norm_ffn2_w, "ffn2_w_in": ffn2_w_in,
            "ffn2_w_out": ffn2_w_out, "norm_final_w": norm_final_w}


def reference(x, c, positions, ada_w, ada_b, norm_ffn1_w, ffn1_w_in, ffn1_w_out, norm_mix_w,
              mix_w_in, mix_gate_b, ssd_conv_w, ssd_conv_b, ssd_dt_bias, ssd_a_log, ssd_d,
              ssd_norm_w, ret_norm_w, w_br_ssd, w_br_ret, mix_w_out, norm_ffn2_w, ffn2_w_in,
              ffn2_w_out, norm_final_w):
    h = x
    c_act = jax.nn.silu(c)
    for l in range(DEPTH):
        mod = c_act @ ada_w[l] + ada_b[l]
        sh1, sc1, g1, sh2, sc2, g2, sh3, sc3, g3 = jnp.split(mod, N_MOD, axis=-1)
        u = modulate(rms_norm(h, norm_ffn1_w[l]), sh1, sc1)
        h = h + FFN_RESIDUAL_WEIGHT * g1[:, None, :] * swiglu(u, ffn1_w_in[l], ffn1_w_out[l])
        u = modulate(rms_norm(h, norm_mix_w[l]), sh2, sc2)
        h = h + g2[:, None, :] * hybrid_mixer(u, positions, mix_w_in[l], mix_gate_b[l],
                                              ssd_conv_w[l], ssd_conv_b[l], ssd_dt_bias[l],
                                              ssd_a_log[l], ssd_d[l], ssd_norm_w[l], ret_norm_w[l],
                                              w_br_ssd[l], w_br_ret[l], mix_w_out[l])
        u = modulate(rms_norm(h, norm_ffn2_w[l]), sh3, sc3)
        h = h + FFN_RESIDUAL_WEIGHT * g3[:, None, :] * swiglu(u, ffn2_w_in[l], ffn2_w_out[l])
    return rms_norm(h, norm_final_w)
```

```python
import functools

import jax
import jax.numpy as jnp
from jax import lax
from jax.experimental import pallas as pl
from jax.experimental.pallas import tpu as pltpu

EPS = 1e-6
CHUNK = 128
SSD_HEAD_DIM = 64
SSD_N_HEADS = 32
SSD_N_GROUPS = 4
SSD_HEADS_PER_GROUP = 8
SSD_D_STATE = 128
SSD_D_INNER = 2048
SSD_CONV_WIDTH = 4
RET_N_HEADS = 8
RET_QK_HEAD_DIM = 128
RET_V_HEAD_DIM = 256
RET_QK_DIM = 1024
RET_V_DIM = 2048
ROPE_BASE = 10000.0
FFN_RESIDUAL_WEIGHT = 0.5
N_MOD = 9

LANES = 128
SUBLANES = 8
VMEM_LIMIT_BYTES = 56 * 1024 * 1024

PROJ_Z, PROJ_XS, PROJ_V, PROJ_G = 0, 2048, 4096, 6144
PROJ_BC, PROJ_Q, PROJ_K, PROJ_GS, PROJ_GR = 8192, 9216, 10240, 11264, 12288
PROJ_DIM = 13312

BF16 = jnp.bfloat16
F32 = jnp.float32


def _sigmoid(x):
    return 1.0 / (1.0 + jnp.exp(-x))


def _silu(x):
    return x * _sigmoid(x)


def _rms_normalize(x):
    return x * lax.rsqrt(jnp.mean(x * x, axis=-1, keepdims=True) + EPS)


def _resident(shape):
    nd = len(shape)
    return pl.BlockSpec(shape, lambda *_: (0,) * nd, pipeline_mode=pl.Buffered(1))


def _ada_kernel(c_ref, w_ref, b_ref, o_ref):
    c_act = _silu(c_ref[...])
    o_ref[...] = jnp.dot(c_act.astype(BF16), w_ref[...].astype(BF16),
                         preferred_element_type=F32) + b_ref[...]


def _ada_call(c, w, b):
    bsz, d = c.shape
    n = w.shape[1]
    tn = d
    return pl.pallas_call(
        _ada_kernel,
        grid=(n // tn,),
        in_specs=[pl.BlockSpec((bsz, d), lambda j: (0, 0)),
                  pl.BlockSpec((d, tn), lambda j: (0, j)),
                  pl.BlockSpec((1, tn), lambda j: (0, j))],
        out_specs=pl.BlockSpec((bsz, tn), lambda j: (0, j)),
        out_shape=jax.ShapeDtypeStruct((bsz, n), F32),
        compiler_params=pltpu.CompilerParams(dimension_semantics=("arbitrary",),
                                             vmem_limit_bytes=VMEM_LIMIT_BYTES),
        name="ada_mod",
    )(c, w, b.reshape(1, n))


def _ffn_kernel(*refs, mode, rows, ffn_dim):
    if mode == "mid":
        (x_ref, mod_ref, nw_ref, win_ref, wout_ref, nw2_ref, wdt_ref,
         h_ref, u_ref, dt_ref) = refs
    else:
        x_ref, mod_ref, nw_ref, win_ref, wout_ref, nw2_ref, o_ref = refs
    sh_row, sc_row, g_row = rows
    h = x_ref[...]
    mod = mod_ref[...]
    u = _rms_normalize(h) * nw_ref[...] * (1.0 + mod[sc_row:sc_row + 1]) + mod[sh_row:sh_row + 1]
    gu = jnp.dot(u.astype(BF16), win_ref[...], preferred_element_type=F32)
    act = _silu(gu[:, :ffn_dim]) * gu[:, ffn_dim:]
    y = jnp.dot(act.astype(BF16), wout_ref[...], preferred_element_type=F32)
    h_new = h + FFN_RESIDUAL_WEIGHT * mod[g_row:g_row + 1] * y
    if mode == "mid":
        h_ref[...] = h_new
        u2 = (_rms_normalize(h_new) * nw2_ref[...] * (1.0 + mod[sc_row + 3:sc_row + 4])
              + mod[sh_row + 3:sh_row + 4]).astype(BF16)
        u_ref[...] = u2
        dt_ref[...] = jnp.dot(u2, wdt_ref[...], preferred_element_type=F32)
    else:
        o_ref[...] = _rms_normalize(h_new) * nw2_ref[...]


def _ffn_call(x, mod, nw, w_in, w_out, nw2, w_dt, *, mode, rows, tm):
    bsz, s, d = x.shape
    ffn_dim = w_out.shape[0]
    grid = (bsz, s // tm)
    tok = lambda width: pl.BlockSpec((None, tm, width), lambda b, i: (b, i, 0))
    in_specs = [tok(d),
                pl.BlockSpec((None, N_MOD, d), lambda b, i: (b, 0, 0)),
                _resident((1, d)), _resident(w_in.shape), _resident(w_out.shape), _resident((1, d))]
    args = [x, mod, nw.reshape(1, d), w_in, w_out, nw2.reshape(1, d)]
    if mode == "mid":
        in_specs.append(_resident(w_dt.shape))
        args.append(w_dt)
        out_specs = [tok(d), tok(d), tok(LANES)]
        out_shape = [jax.ShapeDtypeStruct((bsz, s, d), F32),
                     jax.ShapeDtypeStruct((bsz, s, d), BF16),
                     jax.ShapeDtypeStruct((bsz, s, LANES), F32)]
    else:
        out_specs = tok(d)
        out_shape = jax.ShapeDtypeStruct((bsz, s, d), F32)
    return pl.pallas_call(
        functools.partial(_ffn_kernel, mode=mode, rows=rows, ffn_dim=ffn_dim),
        grid=grid, in_specs=in_specs, out_specs=out_specs, out_shape=out_shape,
        compiler_params=pltpu.CompilerParams(dimension_semantics=("arbitrary", "arbitrary"),
                                             vmem_limit_bytes=VMEM_LIMIT_BYTES),
        name="ffn_" + mode,
    )(*args)


def _proj_kernel(u_ref, w_ref, o_ref):
    o_ref[...] = jnp.dot(u_ref[...], w_ref[...], preferred_element_type=F32).astype(o_ref.dtype)


def _proj_call(u, w, *, tm, tn):
    t, d = u.shape
    n = w.shape[1]
    return pl.pallas_call(
        _proj_kernel,
        grid=(t // tm, n // tn),
        in_specs=[pl.BlockSpec((tm, d), lambda i, j: (i, 0)),
                  pl.BlockSpec((d, tn), lambda i, j: (0, j))],
        out_specs=pl.BlockSpec((tm, tn), lambda i, j: (i, j)),
        out_shape=jax.ShapeDtypeStruct((t, n), BF16),
        compiler_params=pltpu.CompilerParams(dimension_semantics=("arbitrary", "arbitrary"),
                                             vmem_limit_bytes=VMEM_LIMIT_BYTES),
        name="mix_proj",
    )(u, w)


def _mixer_kernel(z_ref, xs_ref, v_ref, g_ref, bc_ref, q_ref, k_ref, gs_ref, gr_ref,
                  dt_ref, h_ref, pos_ref, mod_ref,
                  cwx_ref, cwbc_ref, cbx_ref, cbbc_ref, dtb_ref, alog_ref, dskip_ref,
                  snw_ref, rnw_ref, gb_ref, wbs_ref, wbr_ref, wo_ref,
                  invf_ref, qdec_ref, kdec_ref, dmask_ref, sdec_ref,
                  o_ref,
                  xbuf, sstate, rstate, yssd, yret):
    L = CHUNK
    n_x = SSD_D_INNER
    n_gn = SSD_N_GROUPS * SSD_D_STATE
    tail = SUBLANES

    @pl.when(pl.program_id(1) == 0)
    def _():
        xbuf[0:tail, :] = jnp.zeros((tail, xbuf.shape[1]), F32)
        sstate[...] = jnp.zeros(sstate.shape, F32)
        rstate[...] = jnp.zeros(rstate.shape, F32)

    xbuf[tail:tail + L, 0:n_x] = xs_ref[...].astype(F32)
    xbuf[tail:tail + L, n_x:] = bc_ref[...].astype(F32)

    def conv(lo, hi, w_ref, b_ref):
        acc = b_ref[...]
        for j in range(SSD_CONV_WIDTH):
            k = SSD_CONV_WIDTH - 1 - j
            acc = acc + xbuf[tail - j:tail - j + L, lo:hi] * w_ref[k:k + 1, :]
        return _silu(acc)

    xs_c = conv(0, n_x, cwx_ref, cbx_ref)
    bc_c = conv(n_x, n_x + 2 * n_gn, cwbc_ref, cbbc_ref)
    xbuf[0:tail, :] = xbuf[L:L + tail, :]

    dt = jax.nn.softplus(dt_ref[...] + dtb_ref[...])
    a = dt * (-jnp.exp(alog_ref[...]))
    row = lax.broadcasted_iota(jnp.int32, (L, L), 0)
    col = lax.broadcasted_iota(jnp.int32, (L, L), 1)
    causal = row >= col
    tri = jnp.where(causal, 1.0, 0.0).astype(BF16)
    a1 = a.astype(BF16)
    r1 = a - a1.astype(F32)
    a2 = r1.astype(BF16)
    a3 = (r1 - a2.astype(F32)).astype(BF16)
    a_cs = (jnp.dot(tri, a1, preferred_element_type=F32)
            + jnp.dot(tri, a2, preferred_element_type=F32)
            + jnp.dot(tri, a3, preferred_element_type=F32))
    a_cs_t = a_cs.T
    a_end = a_cs[L - 1:L, :]
    w_state = dt * jnp.exp(a_end - a_cs)
    w_off = jnp.exp(a_cs)
    chunk_decay = jnp.exp(a_end)

    lane = lax.broadcasted_iota(jnp.int32, (1, LANES), 1)
    lo_half = lane < SSD_HEAD_DIM

    hp = SSD_HEAD_DIM
    gw = SSD_HEADS_PER_GROUP * hp
    for g in range(SSD_N_GROUPS):
        b_g = bc_c[:, g * SSD_D_STATE:(g + 1) * SSD_D_STATE].astype(BF16)
        c_g = bc_c[:, n_gn + g * SSD_D_STATE:n_gn + (g + 1) * SSD_D_STATE].astype(BF16)
        cb = lax.dot_general(c_g, b_g, (((1,), (1,)), ((), ())), preferred_element_type=F32)
        st_t = sstate[g]
        y_off = jnp.dot(c_g, st_t.astype(BF16), preferred_element_type=F32)
        xw_parts = []
        dec_parts = []
        for pr in range(SSD_HEADS_PER_GROUP // 2):
            h0 = g * SSD_HEADS_PER_GROUP + 2 * pr
            c0 = h0 * hp
            x_pair = xs_c[:, c0:c0 + 2 * hp]
            dt_pair = jnp.where(lo_half, dt[:, h0:h0 + 1], dt[:, h0 + 1:h0 + 2])
            xdt = x_pair * dt_pair
            y_pair = jnp.zeros((L, 2 * hp), F32)
            for e in range(2):
                h = h0 + e
                seg = a_cs[:, h:h + 1] - a_cs_t[h:h + 1, :]
                decay = jnp.exp(jnp.where(causal, seg, -jnp.inf))
                scores = (cb * decay).astype(BF16)
                keep = lo_half if e == 0 else jnp.logical_not(lo_half)
                x_e = jnp.where(keep, xdt, 0.0).astype(BF16)
                y_pair = y_pair + jnp.dot(scores, x_e, preferred_element_type=F32)
            off_pair = jnp.where(lo_half, w_off[:, h0:h0 + 1], w_off[:, h0 + 1:h0 + 2])
            y_pair = (y_pair + y_off[:, 2 * pr * hp:(2 * pr + 2) * hp] * off_pair
                      + x_pair * dskip_ref[:, c0:c0 + 2 * hp])
            yg = y_pair * _silu(z_ref[:, c0:c0 + 2 * hp].astype(F32))
            yssd[:, c0:c0 + 2 * hp] = yg
            ws_pair = jnp.where(lo_half, w_state[:, h0:h0 + 1], w_state[:, h0 + 1:h0 + 2])
            xw_parts.append((x_pair * ws_pair).astype(BF16))
            dec_parts.append(jnp.where(lo_half, chunk_decay[:, h0:h0 + 1],
                                       chunk_decay[:, h0 + 1:h0 + 2]))
        xw = jnp.concatenate(xw_parts, axis=1)
        dec = jnp.concatenate(dec_parts, axis=1)
        sstate[g] = st_t * dec + lax.dot_general(b_g, xw, (((0,), (0,)), ((), ())),
                                                 preferred_element_type=F32)
        yg_all = yssd[:, g * gw:(g + 1) * gw]
        yssd[:, g * gw:(g + 1) * gw] = _rms_normalize(yg_all) * snw_ref[:, g * gw:(g + 1) * gw]

    ang = pos_ref[...].astype(F32) * invf_ref[...]
    cos = jnp.cos(ang)
    sin_signed = jnp.where(lane < RET_QK_HEAD_DIM // 2, -1.0, 1.0) * jnp.sin(ang)
    k_scale = RET_QK_HEAD_DIM ** -0.5
    dk, dv = RET_QK_HEAD_DIM, RET_V_HEAD_DIM
    for h in range(RET_N_HEADS):
        qh = q_ref[:, h * dk:(h + 1) * dk].astype(F32)
        kh = k_ref[:, h * dk:(h + 1) * dk].astype(F32)
        qr = qh * cos + pltpu.roll(qh, dk // 2, 1) * sin_signed
        kr = (kh * cos + pltpu.roll(kh, dk // 2, 1) * sin_signed) * k_scale
        vh = v_ref[:, h * dv:(h + 1) * dv]
        scores = lax.dot_general(qr.astype(BF16), kr.astype(BF16), (((1,), (1,)), ((), ())),
                                 preferred_element_type=F32) * dmask_ref[h]
        st = rstate[h]
        y = (jnp.dot(scores.astype(BF16), vh, preferred_element_type=F32)
             + jnp.dot((qr * qdec_ref[:, h * dk:(h + 1) * dk]).astype(BF16), st.astype(BF16),
                       preferred_element_type=F32))
        kd = (kr * kdec_ref[:, h * dk:(h + 1) * dk]).astype(BF16)
        rstate[h] = st * sdec_ref[h] + lax.dot_general(kd, vh, (((0,), (0,)), ((), ())),
                                                       preferred_element_type=F32)
        yn = _rms_normalize(y) * rnw_ref[:, h * dv:(h + 1) * dv]
        yret[:, h * dv:(h + 1) * dv] = yn * _silu(g_ref[:, h * dv:(h + 1) * dv].astype(F32))

    d = o_ref.shape[-1]
    y_ssd = jnp.dot(yssd[...].astype(BF16), wbs_ref[...], preferred_element_type=F32)
    y_ret = jnp.dot(yret[...].astype(BF16), wbr_ref[...], preferred_element_type=F32)
    gate_s = _sigmoid(gs_ref[...].astype(F32) + gb_ref[:, 0:d])
    gate_r = _sigmoid(gr_ref[...].astype(F32) + gb_ref[:, d:2 * d])
    mix = gate_s * y_ssd + gate_r * y_ret
    out = jnp.dot(mix.astype(BF16), wo_ref[...], preferred_element_type=F32)
    g2 = mod_ref[5:6, :]
    o_ref[...] = h_ref[...] + g2 * out


def _mixer_call(proj, dt_raw, h, pos, mod, consts, params):
    bsz, s, d = h.shape
    L = CHUNK
    nc = s // L
    grid = (bsz, nc)

    def col(width, start):
        blk = start // width
        return pl.BlockSpec((None, L, width), lambda b, c: (b, c, blk))

    in_specs = [col(2048, PROJ_Z), col(2048, PROJ_XS), col(2048, PROJ_V), col(2048, PROJ_G),
                col(1024, PROJ_BC), col(1024, PROJ_Q), col(1024, PROJ_K),
                col(1024, PROJ_GS), col(1024, PROJ_GR),
                pl.BlockSpec((None, L, LANES), lambda b, c: (b, c, 0)),
                pl.BlockSpec((None, L, d), lambda b, c: (b, c, 0)),
                pl.BlockSpec((None, L, 1), lambda b, c: (b, c, 0)),
                pl.BlockSpec((None, N_MOD, d), lambda b, c: (b, 0, 0))]
    args = [proj] * 9 + [dt_raw, h, pos, mod]
    for arr in list(params) + list(consts):
        in_specs.append(_resident(arr.shape))
        args.append(arr)
    n_xbc = SSD_D_INNER + 2 * SSD_N_GROUPS * SSD_D_STATE
    scratch = [pltpu.VMEM((SUBLANES + L, n_xbc), F32),
               pltpu.VMEM((SSD_N_GROUPS, SSD_D_STATE, SSD_HEADS_PER_GROUP * SSD_HEAD_DIM), F32),
               pltpu.VMEM((RET_N_HEADS, RET_QK_HEAD_DIM, RET_V_HEAD_DIM), F32),
               pltpu.VMEM((L, SSD_D_INNER), F32),
               pltpu.VMEM((L, RET_V_DIM), F32)]
    return pl.pallas_call(
        _mixer_kernel,
        grid=grid, in_specs=in_specs,
        out_specs=pl.BlockSpec((None, L, d), lambda b, c: (b, c, 0)),
        out_shape=jax.ShapeDtypeStruct((bsz, s, d), F32),
        scratch_shapes=scratch,
        compiler_params=pltpu.CompilerParams(dimension_semantics=("arbitrary", "arbitrary"),
                                             vmem_limit_bytes=VMEM_LIMIT_BYTES),
        name="mixer",
    )(*args)


def _retention_constants():
    L = CHUNK
    log_gamma = jnp.log1p(-jnp.exp2(-5.0 - jnp.arange(RET_N_HEADS, dtype=F32)))
    idx = jnp.arange(L, dtype=F32)
    rel = idx[:, None] - idx[None, :]
    dmask = jnp.where(rel >= 0, jnp.exp(log_gamma[:, None, None] * jnp.maximum(rel, 0.0)), 0.0)
    q_decay = jnp.exp(log_gamma[None, :] * (idx + 1.0)[:, None])
    k_decay = jnp.exp(log_gamma[None, :] * (L - 1.0 - idx)[:, None])
    qdec = jnp.repeat(q_decay, RET_QK_HEAD_DIM, axis=1)
    kdec = jnp.repeat(k_decay, RET_QK_HEAD_DIM, axis=1)
    sdec = jnp.broadcast_to(jnp.exp(log_gamma * L)[:, None, None], (RET_N_HEADS, 1, RET_V_HEAD_DIM))
    half = RET_QK_HEAD_DIM // 2
    inv_freq = ROPE_BASE ** (-jnp.arange(half, dtype=F32) / half)
    invf = jnp.concatenate([inv_freq, inv_freq]).reshape(1, RET_QK_HEAD_DIM)
    return invf, qdec, kdec, dmask, sdec


def _pad_lanes(v):
    return jnp.pad(v.reshape(1, -1), ((0, 0), (0, LANES - v.shape[-1])))


def kernel(x, c, positions, ada_w, ada_b, norm_ffn1_w, ffn1_w_in, ffn1_w_out, norm_mix_w, mix_w_in, mix_gate_b, ssd_conv_w, ssd_conv_b, ssd_dt_bias, ssd_a_log, ssd_d, ssd_norm_w, ret_norm_w, w_br_ssd, w_br_ret, mix_w_out, norm_ffn2_w, ffn2_w_in, ffn2_w_out, norm_final_w):
    bsz, s, d = x.shape
    depth = ada_w.shape[0]
    assert s % CHUNK == 0 and d == RET_QK_DIM and depth == 1
    tm_ffn = min(512, s)
    tm_proj = min(2048, bsz * s)
    consts = _retention_constants()
    pos = positions.reshape(bsz, s, 1)

    h = x
    for l in range(depth):
        mod = _ada_call(c, ada_w[l], ada_b[l]).reshape(bsz, N_MOD, d)

        w = mix_w_in[l]
        o_z, o_xbc, o_dt = 0, 2048, 5120
        o_q, o_k, o_v, o_g, o_gs, o_gr = 5152, 6176, 7200, 9248, 11296, 12320
        w_proj = jnp.concatenate(
            [w[:, o_z:o_z + 2048], w[:, o_xbc:o_xbc + 2048], w[:, o_v:o_v + 2048],
             w[:, o_g:o_g + 2048], w[:, o_xbc + 2048:o_xbc + 3072], w[:, o_q:o_q + 1024],
             w[:, o_k:o_k + 1024], w[:, o_gs:o_gs + 1024], w[:, o_gr:o_gr + 1024]],
            axis=1).astype(BF16)
        w_dt = jnp.pad(w[:, o_dt:o_dt + SSD_N_HEADS], ((0, 0), (0, LANES - SSD_N_HEADS))).astype(BF16)

        h, u, dt_raw = _ffn_call(h, mod, norm_ffn1_w[l], ffn1_w_in[l].astype(BF16),
                                 ffn1_w_out[l].astype(BF16), norm_mix_w[l], w_dt,
                                 mode="mid", rows=(0, 1, 2), tm=tm_ffn)
        proj = _proj_call(u.reshape(bsz * s, d), w_proj, tm=tm_proj, tn=1024)
        proj = proj.reshape(bsz, s, PROJ_DIM)

        cw, cb = ssd_conv_w[l], ssd_conv_b[l]
        params = (cw[:, :SSD_D_INNER], cw[:, SSD_D_INNER:],
                  cb[:SSD_D_INNER].reshape(1, -1), cb[SSD_D_INNER:].reshape(1, -1),
                  _pad_lanes(ssd_dt_bias[l]), _pad_lanes(ssd_a_log[l]),
                  jnp.repeat(ssd_d[l], SSD_HEAD_DIM).reshape(1, -1),
                  ssd_norm_w[l].reshape(1, -1), ret_norm_w[l].reshape(1, -1),
                  mix_gate_b[l].reshape(1, -1),
                  w_br_ssd[l].astype(BF16), w_br_ret[l].astype(BF16), mix_w_out[l].astype(BF16))
        h = _mixer_call(proj, dt_raw, h, pos, mod, consts, params)

        h = _ffn_call(h, mod, norm_ffn2_w[l], ffn2_w_in[l].astype(BF16),
                      ffn2_w_out[l].astype(BF16), norm_final_w, None,
                      mode="final", rows=(6, 7, 8), tm=tm_ffn)
    return h
```

```python
import functools

import jax
import jax.numpy as jnp
from jax import lax
from jax.experimental import pallas as pl
from jax.experimental.pallas import tpu as pltpu

EPS = 1e-6
CHUNK = 128
SSD_HEAD_DIM = 64
SSD_N_HEADS = 32
SSD_N_GROUPS = 4
SSD_HEADS_PER_GROUP = 8
SSD_D_STATE = 128
SSD_D_INNER = 2048
SSD_CONV_WIDTH = 4
SSD_BC_DIM = 2 * SSD_N_GROUPS * SSD_D_STATE
RET_N_HEADS = 8
RET_QK_HEAD_DIM = 128
RET_V_HEAD_DIM = 256
RET_QK_DIM = 1024
RET_V_DIM = 2048
ROPE_BASE = 10000.0
FFN_RESIDUAL_WEIGHT = 0.5
N_MOD = 9

LANES = 128
SUBLANES = 8
BF16_ROWS = 16
VMEM_LIMIT_BYTES = 56 * 1024 * 1024

BF16 = jnp.bfloat16
F32 = jnp.float32


def _sigmoid(x):
    return 1.0 / (1.0 + jnp.exp(-x))


def _silu(x):
    return x * _sigmoid(x)


def _rms_normalize(x):
    return x * lax.rsqrt(jnp.mean(x * x, axis=-1, keepdims=True) + EPS)


def _resident(shape):
    nd = len(shape)
    return pl.BlockSpec(shape, lambda *_: (0,) * nd, pipeline_mode=pl.Buffered(1))


def _tok(rows, width, col_block=0):
    return pl.BlockSpec((None, rows, width), lambda b, i: (b, i, col_block))


def _params(n_grid_axes=2):
    return pltpu.CompilerParams(dimension_semantics=("arbitrary",) * n_grid_axes,
                                vmem_limit_bytes=VMEM_LIMIT_BYTES)


def _ada_kernel(c_ref, w_ref, b_ref, o_ref):
    c_act = _silu(c_ref[...])
    o_ref[...] = jnp.dot(c_act.astype(BF16), w_ref[...].astype(BF16),
                         preferred_element_type=F32) + b_ref[...]


def _ada_call(c, w, b):
    bsz, d = c.shape
    n = w.shape[1]
    tn = d
    return pl.pallas_call(
        _ada_kernel,
        grid=(n // tn,),
        in_specs=[pl.BlockSpec((bsz, d), lambda j: (0, 0)),
                  pl.BlockSpec((d, tn), lambda j: (0, j)),
                  pl.BlockSpec((1, tn), lambda j: (0, j))],
        out_specs=pl.BlockSpec((bsz, tn), lambda j: (0, j)),
        out_shape=jax.ShapeDtypeStruct((bsz, n), F32),
        compiler_params=_params(1),
        name="ada_mod",
    )(c, w, b.reshape(1, n))


def _ffn_kernel(*refs, mode, rows, ffn_dim):
    if mode == "mid":
        (x_ref, mod_ref, nw_ref, win_ref, wout_ref, nw2_ref, wdt_ref,
         h_ref, u_ref, dt_ref) = refs
    else:
        x_ref, mod_ref, nw_ref, win_ref, wout_ref, nw2_ref, o_ref = refs
    sh_row, sc_row, g_row = rows
    h = x_ref[...]
    mod = mod_ref[...]
    u = _rms_normalize(h) * nw_ref[...] * (1.0 + mod[sc_row:sc_row + 1]) + mod[sh_row:sh_row + 1]
    gu = jnp.dot(u.astype(BF16), win_ref[...], preferred_element_type=F32)
    act = _silu(gu[:, :ffn_dim]) * gu[:, ffn_dim:]
    y = jnp.dot(act.astype(BF16), wout_ref[...], preferred_element_type=F32)
    h_new = h + FFN_RESIDUAL_WEIGHT * mod[g_row:g_row + 1] * y
    if mode == "mid":
        h_ref[...] = h_new
        u2 = (_rms_normalize(h_new) * nw2_ref[...] * (1.0 + mod[sc_row + 3:sc_row + 4])
              + mod[sh_row + 3:sh_row + 4]).astype(BF16)
        u_ref[...] = u2
        dt_ref[...] = jnp.dot(u2, wdt_ref[...], preferred_element_type=F32)
    else:
        o_ref[...] = _rms_normalize(h_new) * nw2_ref[...]


def _ffn_call(x, mod, nw, w_in, w_out, nw2, w_dt, *, mode, rows, tm):
    bsz, s, d = x.shape
    ffn_dim = w_out.shape[0]
    in_specs = [_tok(tm, d), pl.BlockSpec((None, N_MOD, d), lambda b, i: (b, 0, 0)),
                _resident((1, d)), _resident(w_in.shape), _resident(w_out.shape), _resident((1, d))]
    args = [x, mod, nw.reshape(1, d), w_in, w_out, nw2.reshape(1, d)]
    if mode == "mid":
        in_specs.append(_resident(w_dt.shape))
        args.append(w_dt)
        out_specs = [_tok(tm, d), _tok(tm, d), _tok(tm, LANES)]
        out_shape = [jax.ShapeDtypeStruct((bsz, s, d), F32),
                     jax.ShapeDtypeStruct((bsz, s, d), BF16),
                     jax.ShapeDtypeStruct((bsz, s, LANES), F32)]
    else:
        out_specs = _tok(tm, d)
        out_shape = jax.ShapeDtypeStruct((bsz, s, d), F32)
    return pl.pallas_call(
        functools.partial(_ffn_kernel, mode=mode, rows=rows, ffn_dim=ffn_dim),
        grid=(bsz, s // tm), in_specs=in_specs, out_specs=out_specs, out_shape=out_shape,
        compiler_params=_params(), name="ffn_" + mode,
    )(*args)


def _proj_act_kernel(*refs, act, cw):
    if act == "sigmoid_bias":
        u_ref, w_ref, b_ref, o_ref = refs
    else:
        u_ref, w_ref, o_ref = refs
    u = u_ref[...]
    for c0 in range(0, w_ref.shape[1], cw):
        r = jnp.dot(u, w_ref[:, c0:c0 + cw], preferred_element_type=F32)
        if act == "silu":
            r = _silu(r)
        elif act == "sigmoid_bias":
            r = _sigmoid(r + b_ref[:, c0:c0 + cw])
        o_ref[:, c0:c0 + cw] = r.astype(o_ref.dtype)


def _proj_conv_kernel(u_ref, w_ref, cw_ref, cb_ref, o_ref, tail_ref, *, cw):
    tm = u_ref.shape[0]
    head = BF16_ROWS

    @pl.when(pl.program_id(1) == 0)
    def _():
        tail_ref[...] = jnp.zeros(tail_ref.shape, F32)

    u = u_ref[...]
    for c0 in range(0, w_ref.shape[1], cw):
        r = jnp.dot(u, w_ref[:, c0:c0 + cw], preferred_element_type=F32)
        wv = cw_ref[:, c0:c0 + cw]
        bias = cb_ref[:, c0:c0 + cw]
        k_last = SSD_CONV_WIDTH - 1
        acc = bias + r * wv[k_last:k_last + 1]
        for j in range(1, SSD_CONV_WIDTH):
            acc = acc + pltpu.roll(r, j, 0) * wv[k_last - j:k_last - j + 1]
        o_ref[:, c0:c0 + cw] = _silu(acc).astype(o_ref.dtype)
        cat = jnp.concatenate([tail_ref[:, c0:c0 + cw], r[0:head]], axis=0)
        acc_h = bias + r[0:head] * wv[k_last:k_last + 1]
        for j in range(1, SSD_CONV_WIDTH):
            acc_h = acc_h + (pltpu.roll(cat, j, 0)[SUBLANES:SUBLANES + head]
                             * wv[k_last - j:k_last - j + 1])
        o_ref[0:head, c0:c0 + cw] = _silu(acc_h).astype(o_ref.dtype)
        tail_ref[:, c0:c0 + cw] = r[tm - SUBLANES:tm]


def _proj_rotary_kernel(u_ref, w_ref, pos_ref, invf_ref, qdec_ref, kdec_ref,
                        q_ref, qd_ref, k_ref, kd_ref, *, cw):
    tm = u_ref.shape[0]
    dk = RET_QK_HEAD_DIM
    ang = pos_ref[...].astype(F32) * invf_ref[...]
    lane = lax.broadcasted_iota(jnp.int32, (1, dk), 1)
    cos = jnp.cos(ang)
    sin_signed = jnp.where(lane < dk // 2, -1.0, 1.0) * jnp.sin(ang)
    k_scale = dk ** -0.5
    u = u_ref[...]
    pieces = ((0, cos, sin_signed, q_ref, qd_ref, qdec_ref),
              (RET_QK_DIM, cos * k_scale, sin_signed * k_scale, k_ref, kd_ref, kdec_ref))
    for base, cs, sn, o_ref, od_ref, dec_ref in pieces:
        for c0 in range(0, RET_QK_DIM, cw):
            r = jnp.dot(u, w_ref[:, base + c0:base + c0 + cw], preferred_element_type=F32)
            for hh in range(cw // dk):
                lo = c0 + hh * dk
                x = r[:, hh * dk:(hh + 1) * dk]
                xr = x * cs + pltpu.roll(x, dk // 2, 1) * sn
                o_ref[:, lo:lo + dk] = xr.astype(o_ref.dtype)
                xd = xr.reshape(tm // CHUNK, CHUNK, dk) * dec_ref[:, lo:lo + dk][None]
                od_ref[:, lo:lo + dk] = xd.reshape(tm, dk).astype(od_ref.dtype)


def _proj_call(u, w, *, kind, tm, cw, extra=(), n_out=1, pos=None):
    bsz, s, d = u.shape
    n = w.shape[1]
    in_specs = [_tok(tm, d), _resident(w.shape)]
    args = [u, w]
    scratch = []
    if kind == "rotary":
        body = functools.partial(_proj_rotary_kernel, cw=cw)
        in_specs.append(_tok(tm, 1))
        args.append(pos)
        out_w = n // 2
    elif kind == "conv":
        body = functools.partial(_proj_conv_kernel, cw=cw)
        scratch = [pltpu.VMEM((SUBLANES, n), F32)]
        out_w = n
    else:
        body = functools.partial(_proj_act_kernel, act=kind, cw=cw)
        out_w = n
    for arr in extra:
        in_specs.append(_resident(arr.shape))
        args.append(arr)
    out_shape = [jax.ShapeDtypeStruct((bsz, s, out_w), BF16)] * n_out
    out_specs = [_tok(tm, out_w)] * n_out
    res = pl.pallas_call(
        body, grid=(bsz, s // tm), in_specs=in_specs, out_specs=out_specs, out_shape=out_shape,
        scratch_shapes=scratch, compiler_params=_params(), name="proj_" + kind,
    )(*args)
    return res if n_out > 1 else res[0]


def _mixer_kernel(z_ref, g_ref, xs_ref, bc_ref, v_ref, q_ref, qd_ref, k_ref, kd_ref, dt_ref,
                  dtb_ref, alog_ref, dskip_ref, snw_ref, rnw_ref, dmask_ref, sdec_ref, expand_ref,
                  ys_ref, yr_ref, sstate, rstate, *, cps):
    L = CHUNK
    hp = SSD_HEAD_DIM
    gw = SSD_HEADS_PER_GROUP * hp
    n_gn = SSD_N_GROUPS * SSD_D_STATE
    dk, dv = RET_QK_HEAD_DIM, RET_V_HEAD_DIM

    @pl.when(pl.program_id(1) == 0)
    def _():
        sstate[...] = jnp.zeros(sstate.shape, F32)
        rstate[...] = jnp.zeros(rstate.shape, F32)

    row = lax.broadcasted_iota(jnp.int32, (L, L), 0)
    col = lax.broadcasted_iota(jnp.int32, (L, L), 1)
    causal = row >= col
    tri = jnp.where(causal, 1.0, 0.0).astype(BF16)
    lane = lax.broadcasted_iota(jnp.int32, (1, LANES), 1)
    lo_half = lane < hp

    def chunk_body(ci, carry):
        rows = pl.ds(pl.multiple_of(ci * L, L), L)

        dt = jax.nn.softplus(dt_ref[rows, :] + dtb_ref[...])
        a = dt * (-jnp.exp(alog_ref[...]))
        a1 = a.astype(BF16)
        r1 = a - a1.astype(F32)
        a2 = r1.astype(BF16)
        a3 = (r1 - a2.astype(F32)).astype(BF16)
        a_cs = (jnp.dot(tri, a1, preferred_element_type=F32)
                + jnp.dot(tri, a2, preferred_element_type=F32)
                + jnp.dot(tri, a3, preferred_element_type=F32))
        a_end = a_cs[L - 1:L, :]
        src_t = (a_cs - jnp.log(dt)).T
        wts = jnp.concatenate([jnp.exp(a_cs), dt * jnp.exp(a_end - a_cs)], axis=0)
        w_hi = wts.astype(BF16)
        w_lo = (wts - w_hi.astype(F32)).astype(BF16)
        w_split = jnp.concatenate([w_hi, w_lo], axis=1)

        for g in range(SSD_N_GROUPS):
            gc = slice(g * gw, (g + 1) * gw)
            wide = jnp.dot(w_split, expand_ref[:, gc], preferred_element_type=F32)
            w_off = wide[0:L]
            w_state = wide[L:2 * L]
            b_g = bc_ref[rows, g * SSD_D_STATE:(g + 1) * SSD_D_STATE]
            c_g = bc_ref[rows, n_gn + g * SSD_D_STATE:n_gn + (g + 1) * SSD_D_STATE]
            cb = lax.dot_general(c_g, b_g, (((1,), (1,)), ((), ())), preferred_element_type=F32)
            st_t = sstate[g]
            y_off = jnp.dot(c_g, st_t.astype(BF16), preferred_element_type=F32) * w_off
            x_g = xs_ref[rows, gc]
            x_gf = x_g.astype(F32)
            xw = (x_gf * w_state).astype(BF16)
            sstate[g] = st_t * w_off[L - 1:L, :] + lax.dot_general(
                b_g, xw, (((0,), (0,)), ((), ())), preferred_element_type=F32)
            yg_parts = []
            for pr in range(SSD_HEADS_PER_GROUP // 2):
                h0 = g * SSD_HEADS_PER_GROUP + 2 * pr
                pc = slice(2 * pr * hp, (2 * pr + 2) * hp)
                y_heads = []
                for h in (h0, h0 + 1):
                    seg = a_cs[:, h:h + 1] - src_t[h:h + 1, :]
                    scores = (cb * jnp.exp(jnp.where(causal, seg, -jnp.inf))).astype(BF16)
                    y_heads.append(jnp.dot(scores, x_g[:, pc], preferred_element_type=F32))
                c0 = g * gw + 2 * pr * hp
                y_pair = (jnp.where(lo_half, y_heads[0], y_heads[1]) + y_off[:, pc]
                          + x_gf[:, pc] * dskip_ref[:, c0:c0 + 2 * hp])
                yg_parts.append(y_pair * z_ref[rows, c0:c0 + 2 * hp].astype(F32))
            yg = jnp.concatenate(yg_parts, axis=1)
            ys_ref[rows, gc] = (_rms_normalize(yg) * snw_ref[:, gc]).astype(ys_ref.dtype)

        for h in range(RET_N_HEADS):
            vh = v_ref[rows, h * dv:(h + 1) * dv]
            scores = lax.dot_general(q_ref[rows, h * dk:(h + 1) * dk], k_ref[rows, h * dk:(h + 1) * dk],
                                     (((1,), (1,)), ((), ())),
                                     preferred_element_type=F32) * dmask_ref[h]
            st = rstate[h]
            y = (jnp.dot(scores.astype(BF16), vh, preferred_element_type=F32)
                 + jnp.dot(qd_ref[rows, h * dk:(h + 1) * dk], st.astype(BF16),
                           preferred_element_type=F32))
            rstate[h] = st * sdec_ref[h] + lax.dot_general(
                kd_ref[rows, h * dk:(h + 1) * dk], vh, (((0,), (0,)), ((), ())),
                preferred_element_type=F32)
            yn = _rms_normalize(y) * rnw_ref[:, h * dv:(h + 1) * dv]
            yr_ref[rows, h * dv:(h + 1) * dv] = (
                yn * g_ref[rows, h * dv:(h + 1) * dv].astype(F32)).astype(yr_ref.dtype)
        return carry

    lax.fori_loop(0, cps, chunk_body, 0)


def _mixer_call(zg, xbc, v, q, qd, k, kd, dt_raw, params, consts, *, cps):
    bsz, s, _ = v.shape
    lb = cps * CHUNK
    in_specs = [_tok(lb, SSD_D_INNER, 0), _tok(lb, RET_V_DIM, 1),
                _tok(lb, SSD_D_INNER, 0),
                _tok(lb, SSD_BC_DIM, SSD_D_INNER // SSD_BC_DIM),
                _tok(lb, RET_V_DIM), _tok(lb, RET_QK_DIM), _tok(lb, RET_QK_DIM),
                _tok(lb, RET_QK_DIM), _tok(lb, RET_QK_DIM), _tok(lb, LANES)]
    args = [zg, zg, xbc, xbc, v, q, qd, k, kd, dt_raw]
    for arr in list(params) + list(consts):
        in_specs.append(_resident(arr.shape))
        args.append(arr)
    scratch = [pltpu.VMEM((SSD_N_GROUPS, SSD_D_STATE, SSD_HEADS_PER_GROUP * SSD_HEAD_DIM), F32),
               pltpu.VMEM((RET_N_HEADS, RET_QK_HEAD_DIM, RET_V_HEAD_DIM), F32)]
    return pl.pallas_call(
        functools.partial(_mixer_kernel, cps=cps),
        grid=(bsz, s // lb), in_specs=in_specs,
        out_specs=[_tok(lb, SSD_D_INNER), _tok(lb, RET_V_DIM)],
        out_shape=[jax.ShapeDtypeStruct((bsz, s, SSD_D_INNER), BF16),
                   jax.ShapeDtypeStruct((bsz, s, RET_V_DIM), BF16)],
        scratch_shapes=scratch, compiler_params=_params(), name="mixer",
    )(*args)


def _merge_kernel(ys_ref, yr_ref, gate_ref, h_ref, mod_ref, wbs_ref, wbr_ref, wo_ref, o_ref):
    d = o_ref.shape[-1]
    y_ssd = jnp.dot(ys_ref[...], wbs_ref[...], preferred_element_type=F32)
    y_ret = jnp.dot(yr_ref[...], wbr_ref[...], preferred_element_type=F32)
    mix = gate_ref[:, 0:d].astype(F32) * y_ssd + gate_ref[:, d:2 * d].astype(F32) * y_ret
    out = jnp.dot(mix.astype(BF16), wo_ref[...], preferred_element_type=F32)
    o_ref[...] = h_ref[...] + mod_ref[5:6, :] * out


def _merge_call(ys, yr, gates, h, mod, wbs, wbr, wo, *, tm):
    bsz, s, d = h.shape
    in_specs = [_tok(tm, ys.shape[-1]), _tok(tm, yr.shape[-1]), _tok(tm, gates.shape[-1]),
                _tok(tm, d), pl.BlockSpec((None, N_MOD, d), lambda b, i: (b, 0, 0)),
                _resident(wbs.shape), _resident(wbr.shape), _resident(wo.shape)]
    return pl.pallas_call(
        _merge_kernel, grid=(bsz, s // tm), in_specs=in_specs, out_specs=_tok(tm, d),
        out_shape=jax.ShapeDtypeStruct((bsz, s, d), F32),
        compiler_params=_params(), name="merge",
    )(ys, yr, gates, h, mod, wbs, wbr, wo)


def _retention_constants():
    L = CHUNK
    log_gamma = jnp.log1p(-jnp.exp2(-5.0 - jnp.arange(RET_N_HEADS, dtype=F32)))
    idx = jnp.arange(L, dtype=F32)
    rel = idx[:, None] - idx[None, :]
    dmask = jnp.where(rel >= 0, jnp.exp(log_gamma[:, None, None] * jnp.maximum(rel, 0.0)), 0.0)
    q_decay = jnp.exp(log_gamma[None, :] * (idx + 1.0)[:, None])
    k_decay = jnp.exp(log_gamma[None, :] * (L - 1.0 - idx)[:, None])
    qdec = jnp.repeat(q_decay, RET_QK_HEAD_DIM, axis=1)
    kdec = jnp.repeat(k_decay, RET_QK_HEAD_DIM, axis=1)
    sdec = jnp.broadcast_to(jnp.exp(log_gamma * L)[:, None, None], (RET_N_HEADS, 1, RET_V_HEAD_DIM))
    half = RET_QK_HEAD_DIM // 2
    inv_freq = ROPE_BASE ** (-jnp.arange(half, dtype=F32) / half)
    invf = jnp.concatenate([inv_freq, inv_freq]).reshape(1, RET_QK_HEAD_DIM)
    return invf, qdec, kdec, dmask, sdec


def _pad_lanes(v):
    return jnp.pad(v.reshape(1, -1), ((0, 0), (0, LANES - v.shape[-1])))


def kernel(x, c, positions, ada_w, ada_b, norm_ffn1_w, ffn1_w_in, ffn1_w_out, norm_mix_w, mix_w_in, mix_gate_b, ssd_conv_w, ssd_conv_b, ssd_dt_bias, ssd_a_log, ssd_d, ssd_norm_w, ret_norm_w, w_br_ssd, w_br_ret, mix_w_out, norm_ffn2_w, ffn2_w_in, ffn2_w_out, norm_final_w):
    bsz, s, d = x.shape
    assert s % CHUNK == 0 and d == RET_QK_DIM and ada_w.shape[0] == 1
    tm_ffn = min(512, s)
    tm_proj = min(1024, s)
    cps = min(4, s // CHUNK)
    invf, qdec, kdec, dmask, sdec = _retention_constants()
    pos = positions.reshape(bsz, s, 1)
    l = 0

    mod = _ada_call(c, ada_w[l], ada_b[l]).reshape(bsz, N_MOD, d)

    w = mix_w_in[l]
    o_z, o_xbc, o_dt = 0, 2048, 5120
    o_q, o_k, o_v, o_g, o_gates = 5152, 6176, 7200, 9248, 11296
    w_zg = jnp.concatenate([w[:, o_z:o_z + 2048], w[:, o_g:o_g + 2048]], axis=1).astype(BF16)
    w_xbc = w[:, o_xbc:o_xbc + 3072].astype(BF16)
    w_v = w[:, o_v:o_v + 2048].astype(BF16)
    w_qk = w[:, o_q:o_q + 2048].astype(BF16)
    w_gates = w[:, o_gates:o_gates + 2048].astype(BF16)
    w_dt = jnp.pad(w[:, o_dt:o_dt + SSD_N_HEADS], ((0, 0), (0, LANES - SSD_N_HEADS))).astype(BF16)

    h, u, dt_raw = _ffn_call(x, mod, norm_ffn1_w[l], ffn1_w_in[l].astype(BF16),
                             ffn1_w_out[l].astype(BF16), norm_mix_w[l], w_dt,
                             mode="mid", rows=(0, 1, 2), tm=tm_ffn)

    zg = _proj_call(u, w_zg, kind="silu", tm=tm_proj, cw=512)
    xbc = _proj_call(u, w_xbc, kind="conv", tm=tm_proj, cw=512,
                     extra=(ssd_conv_w[l], ssd_conv_b[l].reshape(1, -1)))
    v = _proj_call(u, w_v, kind="none", tm=tm_proj, cw=512)
    q, qd, k, kd = _proj_call(u, w_qk, kind="rotary", tm=tm_proj, cw=512, pos=pos,
                              extra=(invf, qdec, kdec), n_out=4)
    gates = _proj_call(u, w_gates, kind="sigmoid_bias", tm=tm_proj, cw=512,
                       extra=(mix_gate_b[l].reshape(1, -1),))

    params = (_pad_lanes(ssd_dt_bias[l]), _pad_lanes(ssd_a_log[l]),
              jnp.repeat(ssd_d[l], SSD_HEAD_DIM).reshape(1, -1),
              ssd_norm_w[l].reshape(1, -1), ret_norm_w[l].reshape(1, -1))
    head_of_lane = jnp.arange(SSD_D_INNER) // SSD_HEAD_DIM
    expand = (jnp.arange(LANES)[:, None] == head_of_lane[None, :]).astype(BF16)
    expand = jnp.concatenate([expand, expand], axis=0)
    ys, yr = _mixer_call(zg, xbc, v, q, qd, k, kd, dt_raw, params, (dmask, sdec, expand), cps=cps)

    h = _merge_call(ys, yr, gates, h, mod, w_br_ssd[l].astype(BF16), w_br_ret[l].astype(BF16),
                    mix_w_out[l].astype(BF16), tm=tm_ffn)

    return _ffn_call(h, mod, norm_ffn2_w[l], ffn2_w_in[l].astype(BF16),
                     ffn2_w_out[l].astype(BF16), norm_final_w, None,
                     mode="final", rows=(6, 7, 8), tm=tm_ffn)
```

```python
import functools

import jax
import jax.numpy as jnp
from jax import lax
from jax.experimental import pallas as pl
from jax.experimental.pallas import tpu as pltpu

EPS = 1e-6
CHUNK = 128
SSD_HEAD_DIM = 64
SSD_N_HEADS = 32
SSD_N_GROUPS = 4
SSD_HEADS_PER_GROUP = 8
SSD_D_STATE = 128
SSD_D_INNER = 2048
SSD_CONV_WIDTH = 4
SSD_BC_DIM = 2 * SSD_N_GROUPS * SSD_D_STATE
RET_N_HEADS = 8
RET_QK_HEAD_DIM = 128
RET_V_HEAD_DIM = 256
RET_QK_DIM = 1024
RET_V_DIM = 2048
ROPE_BASE = 10000.0
FFN_RESIDUAL_WEIGHT = 0.5
N_MOD = 9

LANES = 128
SUBLANES = 8
BF16_ROWS = 16
VMEM_LIMIT_BYTES = 56 * 1024 * 1024

BF16 = jnp.bfloat16
F32 = jnp.float32


def _sigmoid(x):
    return 0.5 * jnp.tanh(0.5 * x) + 0.5


def _silu(x):
    hx = 0.5 * x
    return hx * (1.0 + jnp.tanh(hx))


def _rms_normalize(x):
    return x * lax.rsqrt(jnp.mean(x * x, axis=-1, keepdims=True) + EPS)


def _resident(shape):
    nd = len(shape)
    return pl.BlockSpec(shape, lambda *_: (0,) * nd, pipeline_mode=pl.Buffered(1))


def _tok(rows, width, col_block=0):
    return pl.BlockSpec((None, rows, width), lambda b, i: (b, i, col_block))


def _params(n_grid_axes=2):
    return pltpu.CompilerParams(dimension_semantics=("arbitrary",) * n_grid_axes,
                                vmem_limit_bytes=VMEM_LIMIT_BYTES)


def _ada_kernel(c_ref, w_ref, b_ref, o_ref):
    c_act = _silu(c_ref[...])
    o_ref[...] = jnp.dot(c_act.astype(BF16), w_ref[...].astype(BF16),
                         preferred_element_type=F32) + b_ref[...]


def _ada_call(c, w, b):
    bsz, d = c.shape
    n = w.shape[1]
    tn = d
    return pl.pallas_call(
        _ada_kernel,
        grid=(n // tn,),
        in_specs=[pl.BlockSpec((bsz, d), lambda j: (0, 0)),
                  pl.BlockSpec((d, tn), lambda j: (0, j)),
                  pl.BlockSpec((1, tn), lambda j: (0, j))],
        out_specs=pl.BlockSpec((bsz, tn), lambda j: (0, j)),
        out_shape=jax.ShapeDtypeStruct((bsz, n), F32),
        compiler_params=_params(1),
        name="ada_mod",
    )(c, w, b.reshape(1, n))


def _ffn_kernel(*refs, mode, rows, ffn_dim):
    if mode == "mid":
        (x_ref, mod_ref, nw_ref, win_ref, wout_ref, nw2_ref, wdt_ref,
         h_ref, u_ref, dt_ref) = refs
    else:
        x_ref, mod_ref, nw_ref, win_ref, wout_ref, nw2_ref, o_ref = refs
    sh_row, sc_row, g_row = rows
    h = x_ref[...]
    mod = mod_ref[...]
    u = _rms_normalize(h) * nw_ref[...] * (1.0 + mod[sc_row:sc_row + 1]) + mod[sh_row:sh_row + 1]
    gu = jnp.dot(u.astype(BF16), win_ref[...], preferred_element_type=F32)
    act = _silu(gu[:, :ffn_dim]) * gu[:, ffn_dim:]
    y = jnp.dot(act.astype(BF16), wout_ref[...], preferred_element_type=F32)
    h_new = h + FFN_RESIDUAL_WEIGHT * mod[g_row:g_row + 1] * y
    if mode == "mid":
        h_ref[...] = h_new
        u2 = (_rms_normalize(h_new) * nw2_ref[...] * (1.0 + mod[sc_row + 3:sc_row + 4])
              + mod[sh_row + 3:sh_row + 4]).astype(BF16)
        u_ref[...] = u2
        dt_ref[...] = jnp.dot(u2, wdt_ref[...], preferred_element_type=F32)
    else:
        o_ref[...] = _rms_normalize(h_new) * nw2_ref[...]


def _ffn_call(x, mod, nw, w_in, w_out, nw2, w_dt, *, mode, rows, tm):
    bsz, s, d = x.shape
    ffn_dim = w_out.shape[0]
    in_specs = [_tok(tm, d), pl.BlockSpec((None, N_MOD, d), lambda b, i: (b, 0, 0)),
                _resident((1, d)), _resident(w_in.shape), _resident(w_out.shape), _resident((1, d))]
    args = [x, mod, nw.reshape(1, d), w_in, w_out, nw2.reshape(1, d)]
    if mode == "mid":
        in_specs.append(_resident(w_dt.shape))
        args.append(w_dt)
        out_specs = [_tok(tm, d), _tok(tm, d), _tok(tm, LANES)]
        out_shape = [jax.ShapeDtypeStruct((bsz, s, d), F32),
                     jax.ShapeDtypeStruct((bsz, s, d), BF16),
                     jax.ShapeDtypeStruct((bsz, s, LANES), F32)]
    else:
        out_specs = _tok(tm, d)
        out_shape = jax.ShapeDtypeStruct((bsz, s, d), F32)
    return pl.pallas_call(
        functools.partial(_ffn_kernel, mode=mode, rows=rows, ffn_dim=ffn_dim),
        grid=(bsz, s // tm), in_specs=in_specs, out_specs=out_specs, out_shape=out_shape,
        compiler_params=_params(), name="ffn_" + mode,
    )(*args)


def _proj_act_kernel(*refs, act, cw):
    if act == "sigmoid_bias":
        u_ref, w_ref, b_ref, o_ref = refs
    else:
        u_ref, w_ref, o_ref = refs
    u = u_ref[...]
    for c0 in range(0, w_ref.shape[1], cw):
        r = jnp.dot(u, w_ref[:, c0:c0 + cw], preferred_element_type=F32)
        if act == "silu":
            r = _silu(r)
        elif act == "sigmoid_bias":
            r = _sigmoid(r + b_ref[:, c0:c0 + cw])
        o_ref[:, c0:c0 + cw] = r.astype(o_ref.dtype)


def _proj_conv_kernel(u_ref, w_ref, cw_ref, cb_ref, o_ref, tail_ref, *, cw):
    tm = u_ref.shape[0]
    assert SSD_CONV_WIDTH == 4

    @pl.when(pl.program_id(1) == 0)
    def _():
        tail_ref[...] = jnp.zeros(tail_ref.shape, F32)

    sub = lax.broadcasted_iota(jnp.int32, (1, SUBLANES, 1), 1)

    def shift(x3, first, j):
        n = x3.shape[0]
        rot = pltpu.roll(jnp.concatenate([first, x3], axis=0), j, 1)
        return jnp.where(sub < j, rot[0:n], rot[1:n + 1])

    u = u_ref[...]
    for c0 in range(0, w_ref.shape[1], cw):
        r = jnp.dot(u, w_ref[:, c0:c0 + cw], preferred_element_type=F32)
        r3 = r.reshape(tm // SUBLANES, SUBLANES, cw)
        w0, w1, w2, w3 = (cw_ref[k:k + 1, c0:c0 + cw].reshape(1, 1, cw) for k in range(4))
        bias = cb_ref[:, c0:c0 + cw].reshape(1, 1, cw)
        prev = tail_ref[:, c0:c0 + cw].reshape(1, SUBLANES, cw)
        s1 = shift(r3, prev, 1)
        b_term = w1 * r3 + w0 * s1
        b_prev = w1 * prev + w0 * shift(prev, prev, 1)
        acc = bias + w3 * r3 + w2 * s1 + shift(b_term, b_prev, 2)
        o_ref[:, c0:c0 + cw] = _silu(acc).reshape(tm, cw).astype(o_ref.dtype)
        tail_ref[:, c0:c0 + cw] = r[tm - SUBLANES:tm]


def _proj_rotary_kernel(u_ref, w_ref, pos_ref, invf_ref, qdec_ref, kdec_ref,
                        q_ref, qd_ref, k_ref, kd_ref, *, cw):
    tm = u_ref.shape[0]
    dk = RET_QK_HEAD_DIM
    lane = lax.broadcasted_iota(jnp.int32, (1, dk), 1)
    first_half = lane < dk // 2
    pos = pos_ref[...].astype(F32)
    ang = jnp.where(first_half, pos[0:tm // 2], pos[tm // 2:tm]) * invf_ref[...]
    cos_p, sin_p = jnp.cos(ang), jnp.sin(ang)
    cos_q, sin_q = pltpu.roll(cos_p, dk // 2, 1), pltpu.roll(sin_p, dk // 2, 1)
    cos = jnp.concatenate([jnp.where(first_half, cos_p, cos_q),
                           jnp.where(first_half, cos_q, cos_p)], axis=0)
    sin_signed = jnp.concatenate([jnp.where(first_half, -sin_p, sin_q),
                                  jnp.where(first_half, -sin_q, sin_p)], axis=0)
    k_scale = dk ** -0.5
    u = u_ref[...]
    pieces = ((0, cos, sin_signed, q_ref, qd_ref, qdec_ref),
              (RET_QK_DIM, cos * k_scale, sin_signed * k_scale, k_ref, kd_ref, kdec_ref))
    for base, cs, sn, o_ref, od_ref, dec_ref in pieces:
        for c0 in range(0, RET_QK_DIM, cw):
            r = jnp.dot(u, w_ref[:, base + c0:base + c0 + cw], preferred_element_type=F32)
            for hh in range(cw // dk):
                lo = c0 + hh * dk
                x = r[:, hh * dk:(hh + 1) * dk]
                xr = x * cs + pltpu.roll(x, dk // 2, 1) * sn
                o_ref[:, lo:lo + dk] = xr.astype(o_ref.dtype)
                xd = xr.reshape(tm // CHUNK, CHUNK, dk) * dec_ref[:, lo:lo + dk][None]
                od_ref[:, lo:lo + dk] = xd.reshape(tm, dk).astype(od_ref.dtype)


def _proj_call(u, w, *, kind, tm, cw, extra=(), n_out=1, pos=None):
    bsz, s, d = u.shape
    n = w.shape[1]
    in_specs = [_tok(tm, d), _resident(w.shape)]
    args = [u, w]
    scratch = []
    if kind == "rotary":
        body = functools.partial(_proj_rotary_kernel, cw=cw)
        in_specs.append(_tok(tm, 1))
        args.append(pos)
        out_w = n // 2
    elif kind == "conv":
        body = functools.partial(_proj_conv_kernel, cw=cw)
        scratch = [pltpu.VMEM((SUBLANES, n), F32)]
        out_w = n
    else:
        body = functools.partial(_proj_act_kernel, act=kind, cw=cw)
        out_w = n
    for arr in extra:
        in_specs.append(_resident(arr.shape))
        args.append(arr)
    out_shape = [jax.ShapeDtypeStruct((bsz, s, out_w), BF16)] * n_out
    out_specs = [_tok(tm, out_w)] * n_out
    res = pl.pallas_call(
        body, grid=(bsz, s // tm), in_specs=in_specs, out_specs=out_specs, out_shape=out_shape,
        scratch_shapes=scratch, compiler_params=_params(), name="proj_" + kind,
    )(*args)
    return res if n_out > 1 else res[0]


def _mixer_kernel(z_ref, g_ref, xs_ref, bc_ref, v_ref, q_ref, qd_ref, k_ref, kd_ref, dt_ref,
                  dtb_ref, alog_ref, dskip_ref, snw_ref, rnw_ref, dmask_ref, sdec_ref, expand_ref,
                  ys_ref, yr_ref, sstate, rstate, acs_s, srct_s, wsplit_s, *, cps):
    L = CHUNK
    hp = SSD_HEAD_DIM
    gw = SSD_HEADS_PER_GROUP * hp
    n_gn = SSD_N_GROUPS * SSD_D_STATE
    dk, dv = RET_QK_HEAD_DIM, RET_V_HEAD_DIM

    @pl.when(pl.program_id(1) == 0)
    def _():
        sstate[...] = jnp.zeros(sstate.shape, F32)
        rstate[...] = jnp.zeros(rstate.shape, F32)

    row = lax.broadcasted_iota(jnp.int32, (L, L), 0)
    col = lax.broadcasted_iota(jnp.int32, (L, L), 1)
    causal = row >= col
    tri = jnp.where(causal, 1.0, 0.0).astype(BF16)
    lane = lax.broadcasted_iota(jnp.int32, (1, LANES), 1)
    lo_half = lane < hp

    dt_all = jax.nn.softplus(dt_ref[...] + dtb_ref[...])
    a_all = dt_all * (-jnp.exp(alog_ref[...]))
    log_dt_all = jnp.log(dt_all)
    a1 = a_all.astype(BF16)
    r1 = a_all - a1.astype(F32)
    a2 = r1.astype(BF16)
    a3 = (r1 - a2.astype(F32)).astype(BF16)
    for ci in range(cps):
        sl = slice(ci * L, (ci + 1) * L)
        a_cs = (jnp.dot(tri, a1[sl], preferred_element_type=F32)
                + jnp.dot(tri, a2[sl], preferred_element_type=F32)
                + jnp.dot(tri, a3[sl], preferred_element_type=F32))
        a_end = a_cs[L - 1:L, :]
        acs_s[sl, :] = a_cs
        srct_s[ci] = (a_cs - log_dt_all[sl]).T
        wts = jnp.concatenate([jnp.exp(a_cs), dt_all[sl] * jnp.exp(a_end - a_cs)], axis=0)
        w_hi = wts.astype(BF16)
        w_lo = (wts - w_hi.astype(F32)).astype(BF16)
        wsplit_s[2 * ci * L:2 * (ci + 1) * L, :] = jnp.concatenate([w_hi, w_lo], axis=1)

    def chunk_body(ci, carry):
        rows = pl.ds(pl.multiple_of(ci * L, L), L)
        a_cs = acs_s[rows, :]
        src_t = srct_s[ci]
        w_split = wsplit_s[pl.ds(pl.multiple_of(2 * ci * L, 2 * L), 2 * L), :]

        for g in range(SSD_N_GROUPS):
            gc = slice(g * gw, (g + 1) * gw)
            wide = jnp.dot(w_split, expand_ref[:, gc], preferred_element_type=F32)
            w_off = wide[0:L]
            w_state = wide[L:2 * L]
            b_g = bc_ref[rows, g * SSD_D_STATE:(g + 1) * SSD_D_STATE]
            c_g = bc_ref[rows, n_gn + g * SSD_D_STATE:n_gn + (g + 1) * SSD_D_STATE]
            cb = lax.dot_general(c_g, b_g, (((1,), (1,)), ((), ())), preferred_element_type=F32)
            st_t = sstate[g]
            y_off = jnp.dot(c_g, st_t.astype(BF16), preferred_element_type=F32) * w_off
            x_g = xs_ref[rows, gc]
            x_gf = x_g.astype(F32)
            xw = (x_gf * w_state).astype(BF16)
            sstate[g] = st_t * w_off[L - 1:L, :] + lax.dot_general(
                b_g, xw, (((0,), (0,)), ((), ())), preferred_element_type=F32)
            yg_parts = []
            for pr in range(SSD_HEADS_PER_GROUP // 2):
                h0 = g * SSD_HEADS_PER_GROUP + 2 * pr
                pc = slice(2 * pr * hp, (2 * pr + 2) * hp)
                y_heads = []
                for h in (h0, h0 + 1):
                    seg = a_cs[:, h:h + 1] - src_t[h:h + 1, :]
                    scores = (cb * jnp.exp(jnp.where(causal, seg, -jnp.inf))).astype(BF16)
                    y_heads.append(jnp.dot(scores, x_g[:, pc], preferred_element_type=F32))
                c0 = g * gw + 2 * pr * hp
                y_pair = (jnp.where(lo_half, y_heads[0], y_heads[1]) + y_off[:, pc]
                          + x_gf[:, pc] * dskip_ref[:, c0:c0 + 2 * hp])
                yg_parts.append(y_pair * z_ref[rows, c0:c0 + 2 * hp].astype(F32))
            yg = jnp.concatenate(yg_parts, axis=1)
            ys_ref[rows, gc] = (_rms_normalize(yg) * snw_ref[:, gc]).astype(ys_ref.dtype)

        for h in range(RET_N_HEADS):
            vh = v_ref[rows, h * dv:(h + 1) * dv]
            scores = lax.dot_general(q_ref[rows, h * dk:(h + 1) * dk], k_ref[rows, h * dk:(h + 1) * dk],
                                     (((1,), (1,)), ((), ())),
                                     preferred_element_type=F32) * dmask_ref[h]
            st = rstate[h]
            y = (jnp.dot(scores.astype(BF16), vh, preferred_element_type=F32)
                 + jnp.dot(qd_ref[rows, h * dk:(h + 1) * dk], st.astype(BF16),
                           preferred_element_type=F32))
            rstate[h] = st * sdec_ref[h] + lax.dot_general(
                kd_ref[rows, h * dk:(h + 1) * dk], vh, (((0,), (0,)), ((), ())),
                preferred_element_type=F32)
            yn = _rms_normalize(y) * rnw_ref[:, h * dv:(h + 1) * dv]
            yr_ref[rows, h * dv:(h + 1) * dv] = (
                yn * g_ref[rows, h * dv:(h + 1) * dv].astype(F32)).astype(yr_ref.dtype)
        return carry

    lax.fori_loop(0, cps, chunk_body, 0, unroll=True)


def _mixer_call(zg, xbc, v, q, qd, k, kd, dt_raw, params, consts, *, cps):
    bsz, s, _ = v.shape
    lb = cps * CHUNK
    in_specs = [_tok(lb, SSD_D_INNER, 0), _tok(lb, RET_V_DIM, 1),
                _tok(lb, SSD_D_INNER, 0),
                _tok(lb, SSD_BC_DIM, SSD_D_INNER // SSD_BC_DIM),
                _tok(lb, RET_V_DIM), _tok(lb, RET_QK_DIM), _tok(lb, RET_QK_DIM),
                _tok(lb, RET_QK_DIM), _tok(lb, RET_QK_DIM), _tok(lb, LANES)]
    args = [zg, zg, xbc, xbc, v, q, qd, k, kd, dt_raw]
    for arr in list(params) + list(consts):
        in_specs.append(_resident(arr.shape))
        args.append(arr)
    scratch = [pltpu.VMEM((SSD_N_GROUPS, SSD_D_STATE, SSD_HEADS_PER_GROUP * SSD_HEAD_DIM), F32),
               pltpu.VMEM((RET_N_HEADS, RET_QK_HEAD_DIM, RET_V_HEAD_DIM), F32),
               pltpu.VMEM((lb, LANES), F32),
               pltpu.VMEM((cps, LANES, CHUNK), F32),
               pltpu.VMEM((2 * lb, 2 * LANES), BF16)]
    return pl.pallas_call(
        functools.partial(_mixer_kernel, cps=cps),
        grid=(bsz, s // lb), in_specs=in_specs,
        out_specs=[_tok(lb, SSD_D_INNER), _tok(lb, RET_V_DIM)],
        out_shape=[jax.ShapeDtypeStruct((bsz, s, SSD_D_INNER), BF16),
                   jax.ShapeDtypeStruct((bsz, s, RET_V_DIM), BF16)],
        scratch_shapes=scratch, compiler_params=_params(), name="mixer",
    )(*args)


def _merge_kernel(ys_ref, yr_ref, gate_ref, h_ref, mod_ref, wbs_ref, wbr_ref, wo_ref, o_ref):
    d = o_ref.shape[-1]
    y_ssd = jnp.dot(ys_ref[...], wbs_ref[...], preferred_element_type=F32)
    y_ret = jnp.dot(yr_ref[...], wbr_ref[...], preferred_element_type=F32)
    mix = gate_ref[:, 0:d].astype(F32) * y_ssd + gate_ref[:, d:2 * d].astype(F32) * y_ret
    out = jnp.dot(mix.astype(BF16), wo_ref[...], preferred_element_type=F32)
    o_ref[...] = h_ref[...] + mod_ref[5:6, :] * out


def _merge_call(ys, yr, gates, h, mod, wbs, wbr, wo, *, tm):
    bsz, s, d = h.shape
    in_specs = [_tok(tm, ys.shape[-1]), _tok(tm, yr.shape[-1]), _tok(tm, gates.shape[-1]),
                _tok(tm, d), pl.BlockSpec((None, N_MOD, d), lambda b, i: (b, 0, 0)),
                _resident(wbs.shape), _resident(wbr.shape), _resident(wo.shape)]
    return pl.pallas_call(
        _merge_kernel, grid=(bsz, s // tm), in_specs=in_specs, out_specs=_tok(tm, d),
        out_shape=jax.ShapeDtypeStruct((bsz, s, d), F32),
        compiler_params=_params(), name="merge",
    )(ys, yr, gates, h, mod, wbs, wbr, wo)


def _retention_constants():
    L = CHUNK
    log_gamma = jnp.log1p(-jnp.exp2(-5.0 - jnp.arange(RET_N_HEADS, dtype=F32)))
    idx = jnp.arange(L, dtype=F32)
    rel = idx[:, None] - idx[None, :]
    dmask = jnp.where(rel >= 0, jnp.exp(log_gamma[:, None, None] * jnp.maximum(rel, 0.0)), 0.0)
    q_decay = jnp.exp(log_gamma[None, :] * (idx + 1.0)[:, None])
    k_decay = jnp.exp(log_gamma[None, :] * (L - 1.0 - idx)[:, None])
    qdec = jnp.repeat(q_decay, RET_QK_HEAD_DIM, axis=1)
    kdec = jnp.repeat(k_decay, RET_QK_HEAD_DIM, axis=1)
    sdec = jnp.broadcast_to(jnp.exp(log_gamma * L)[:, None, None], (RET_N_HEADS, 1, RET_V_HEAD_DIM))
    half = RET_QK_HEAD_DIM // 2
    inv_freq = ROPE_BASE ** (-jnp.arange(half, dtype=F32) / half)
    invf = jnp.concatenate([inv_freq, inv_freq]).reshape(1, RET_QK_HEAD_DIM)
    return invf, qdec, kdec, dmask, sdec


def _pad_lanes(v):
    return jnp.pad(v.reshape(1, -1), ((0, 0), (0, LANES - v.shape[-1])))


def kernel(x, c, positions, ada_w, ada_b, norm_ffn1_w, ffn1_w_in, ffn1_w_out, norm_mix_w, mix_w_in, mix_gate_b, ssd_conv_w, ssd_conv_b, ssd_dt_bias, ssd_a_log, ssd_d, ssd_norm_w, ret_norm_w, w_br_ssd, w_br_ret, mix_w_out, norm_ffn2_w, ffn2_w_in, ffn2_w_out, norm_final_w):
    bsz, s, d = x.shape
    assert s % CHUNK == 0 and d == RET_QK_DIM and ada_w.shape[0] == 1
    tm_ffn = min(512, s)
    tm_proj = min(1024, s)
    cps = min(4, s // CHUNK)
    invf, qdec, kdec, dmask, sdec = _retention_constants()
    pos = positions.reshape(bsz, s, 1)
    l = 0

    mod = _ada_call(c, ada_w[l], ada_b[l]).reshape(bsz, N_MOD, d)

    w = mix_w_in[l]
    o_z, o_xbc, o_dt = 0, 2048, 5120
    o_q, o_k, o_v, o_g, o_gates = 5152, 6176, 7200, 9248, 11296
    w_zg = jnp.concatenate([w[:, o_z:o_z + 2048], w[:, o_g:o_g + 2048]], axis=1).astype(BF16)
    w_xbc = w[:, o_xbc:o_xbc + 3072].astype(BF16)
    w_v = w[:, o_v:o_v + 2048].astype(BF16)
    w_qk = w[:, o_q:o_q + 2048].astype(BF16)
    w_gates = w[:, o_gates:o_gates + 2048].astype(BF16)
    w_dt = jnp.pad(w[:, o_dt:o_dt + SSD_N_HEADS], ((0, 0), (0, LANES - SSD_N_HEADS))).astype(BF16)

    h, u, dt_raw = _ffn_call(x, mod, norm_ffn1_w[l], ffn1_w_in[l].astype(BF16),
                             ffn1_w_out[l].astype(BF16), norm_mix_w[l], w_dt,
                             mode="mid", rows=(0, 1, 2), tm=tm_ffn)

    zg = _proj_call(u, w_zg, kind="silu", tm=tm_proj, cw=512)
    xbc = _proj_call(u, w_xbc, kind="conv", tm=tm_proj, cw=512,
                     extra=(ssd_conv_w[l], ssd_conv_b[l].reshape(1, -1)))
    v = _proj_call(u, w_v, kind="none", tm=tm_proj, cw=512)
    q, qd, k, kd = _proj_call(u, w_qk, kind="rotary", tm=tm_proj, cw=512, pos=pos,
                              extra=(invf, qdec, kdec), n_out=4)
    gates = _proj_call(u, w_gates, kind="sigmoid_bias", tm=tm_proj, cw=512,
                       extra=(mix_gate_b[l].reshape(1, -1),))

    params = (_pad_lanes(ssd_dt_bias[l]), _pad_lanes(ssd_a_log[l]),
              jnp.repeat(ssd_d[l], SSD_HEAD_DIM).reshape(1, -1),
              ssd_norm_w[l].reshape(1, -1), ret_norm_w[l].reshape(1, -1))
    head_of_lane = jnp.arange(SSD_D_INNER) // SSD_HEAD_DIM
    expand = (jnp.arange(LANES)[:, None] == head_of_lane[None, :]).astype(BF16)
    expand = jnp.concatenate([expand, expand], axis=0)
    ys, yr = _mixer_call(zg, xbc, v, q, qd, k, kd, dt_raw, params, (dmask, sdec, expand), cps=cps)

    h = _merge_call(ys, yr, gates, h, mod, w_br_ssd[l].astype(BF16), w_br_ret[l].astype(BF16),
                    mix_w_out[l].astype(BF16), tm=tm_ffn)

    return _ffn_call(h, mod, norm_ffn2_w[l], ffn2_w_in[l].astype(BF16),
                     ffn2_w_out[l].astype(BF16), norm_final_w, None,
                     mode="final", rows=(6, 7, 8), tm=tm_ffn)
```

```python
import functools

import jax
import jax.numpy as jnp
from jax import lax
from jax.experimental import pallas as pl
from jax.experimental.pallas import tpu as pltpu

EPS = 1e-6
LOG2E = 1.4426950408889634
CHUNK = 128
SSD_HEAD_DIM = 64
SSD_N_HEADS = 32
SSD_N_GROUPS = 4
SSD_HEADS_PER_GROUP = 8
SSD_D_STATE = 128
SSD_D_INNER = 2048
SSD_CONV_WIDTH = 4
SSD_BC_DIM = 2 * SSD_N_GROUPS * SSD_D_STATE
RET_N_HEADS = 8
RET_QK_HEAD_DIM = 128
RET_V_HEAD_DIM = 256
RET_QK_DIM = 1024
RET_V_DIM = 2048
ROPE_BASE = 10000.0
FFN_RESIDUAL_WEIGHT = 0.5
N_MOD = 9

LANES = 128
SUBLANES = 8
BF16_ROWS = 16
VMEM_LIMIT_BYTES = 56 * 1024 * 1024

BF16 = jnp.bfloat16
F32 = jnp.float32


def _sigmoid(x):
    return 0.5 * jnp.tanh(0.5 * x) + 0.5


def _silu(x):
    hx = 0.5 * x
    return hx * (1.0 + jnp.tanh(hx))


def _rms_normalize(x):
    return x * lax.rsqrt(jnp.mean(x * x, axis=-1, keepdims=True) + EPS)


def _resident(shape):
    nd = len(shape)
    return pl.BlockSpec(shape, lambda *_: (0,) * nd, pipeline_mode=pl.Buffered(1))


def _tok(rows, width, col_block=0):
    return pl.BlockSpec((None, rows, width), lambda b, i: (b, i, col_block))


def _params(n_grid_axes=2):
    return pltpu.CompilerParams(dimension_semantics=("arbitrary",) * n_grid_axes,
                                vmem_limit_bytes=VMEM_LIMIT_BYTES)


def _ada_kernel(c_ref, w_ref, b_ref, o_ref):
    c_act = _silu(c_ref[...])
    o_ref[...] = jnp.dot(c_act.astype(BF16), w_ref[...].astype(BF16),
                         preferred_element_type=F32) + b_ref[...]


def _ada_call(c, w, b):
    bsz, d = c.shape
    n = w.shape[1]
    tn = d
    return pl.pallas_call(
        _ada_kernel,
        grid=(n // tn,),
        in_specs=[pl.BlockSpec((bsz, d), lambda j: (0, 0)),
                  pl.BlockSpec((d, tn), lambda j: (0, j)),
                  pl.BlockSpec((1, tn), lambda j: (0, j))],
        out_specs=pl.BlockSpec((bsz, tn), lambda j: (0, j)),
        out_shape=jax.ShapeDtypeStruct((bsz, n), F32),
        compiler_params=_params(1),
        name="ada_mod",
    )(c, w, b.reshape(1, n))


def _ffn_kernel(*refs, mode, rows, ffn_dim):
    if mode == "mid":
        (x_ref, mod_ref, nw_ref, win_ref, wout_ref, nw2_ref, wdt_ref,
         h_ref, u_ref, dt_ref) = refs
    else:
        x_ref, mod_ref, nw_ref, win_ref, wout_ref, nw2_ref, o_ref = refs
    sh_row, sc_row, g_row = rows
    h = x_ref[...]
    mod = mod_ref[...]
    u = _rms_normalize(h) * nw_ref[...] * (1.0 + mod[sc_row:sc_row + 1]) + mod[sh_row:sh_row + 1]
    gu = jnp.dot(u.astype(BF16), win_ref[...], preferred_element_type=F32)
    act = _silu(gu[:, :ffn_dim]) * gu[:, ffn_dim:]
    y = jnp.dot(act.astype(BF16), wout_ref[...], preferred_element_type=F32)
    h_new = h + FFN_RESIDUAL_WEIGHT * mod[g_row:g_row + 1] * y
    if mode == "mid":
        h_ref[...] = h_new
        u2 = (_rms_normalize(h_new) * nw2_ref[...] * (1.0 + mod[sc_row + 3:sc_row + 4])
              + mod[sh_row + 3:sh_row + 4]).astype(BF16)
        u_ref[...] = u2
        dt_ref[...] = jnp.dot(u2, wdt_ref[...], preferred_element_type=F32)
    else:
        o_ref[...] = _rms_normalize(h_new) * nw2_ref[...]


def _ffn_call(x, mod, nw, w_in, w_out, nw2, w_dt, *, mode, rows, tm):
    bsz, s, d = x.shape
    ffn_dim = w_out.shape[0]
    in_specs = [_tok(tm, d), pl.BlockSpec((None, N_MOD, d), lambda b, i: (b, 0, 0)),
                _resident((1, d)), _resident(w_in.shape), _resident(w_out.shape), _resident((1, d))]
    args = [x, mod, nw.reshape(1, d), w_in, w_out, nw2.reshape(1, d)]
    if mode == "mid":
        in_specs.append(_resident(w_dt.shape))
        args.append(w_dt)
        out_specs = [_tok(tm, d), _tok(tm, d), _tok(tm, LANES)]
        out_shape = [jax.ShapeDtypeStruct((bsz, s, d), F32),
                     jax.ShapeDtypeStruct((bsz, s, d), BF16),
                     jax.ShapeDtypeStruct((bsz, s, LANES), F32)]
    else:
        out_specs = _tok(tm, d)
        out_shape = jax.ShapeDtypeStruct((bsz, s, d), F32)
    return pl.pallas_call(
        functools.partial(_ffn_kernel, mode=mode, rows=rows, ffn_dim=ffn_dim),
        grid=(bsz, s // tm), in_specs=in_specs, out_specs=out_specs, out_shape=out_shape,
        compiler_params=_params(), name="ffn_" + mode,
    )(*args)


def _parked_dot(u, w_chunk, buf_ref):
    rows = pl.ds(pl.multiple_of(jnp.minimum(pl.program_id(1), 0), SUBLANES), u.shape[0])
    buf_ref[rows, :] = jnp.dot(u, w_chunk, preferred_element_type=F32)
    return buf_ref[rows, :]


def _proj_act_kernel(*refs, act, cw):
    if act == "sigmoid_bias":
        u_ref, w_ref, b_ref, o_ref, *bufs = refs
    else:
        u_ref, w_ref, o_ref, *bufs = refs
    u = u_ref[...]
    for ci, c0 in enumerate(range(0, w_ref.shape[1], cw)):
        r = _parked_dot(u, w_ref[:, c0:c0 + cw], bufs[ci % 2])
        if act == "silu":
            r = _silu(r)
        elif act == "sigmoid_bias":
            r = _sigmoid(r + b_ref[:, c0:c0 + cw])
        o_ref[:, c0:c0 + cw] = r.astype(o_ref.dtype)


def _proj_conv_kernel(u_ref, w_ref, cw_ref, cb_ref, o_ref, tail_ref, *bufs, cw):
    tm = u_ref.shape[0]
    assert SSD_CONV_WIDTH == 4

    @pl.when(pl.program_id(1) == 0)
    def _():
        tail_ref[...] = jnp.zeros(tail_ref.shape, F32)

    sub = lax.broadcasted_iota(jnp.int32, (1, SUBLANES, 1), 1)

    def shift(x3, first, j):
        n = x3.shape[0]
        rot = pltpu.roll(jnp.concatenate([first, x3], axis=0), j, 1)
        return jnp.where(sub < j, rot[0:n], rot[1:n + 1])

    u = u_ref[...]
    for ci, c0 in enumerate(range(0, w_ref.shape[1], cw)):
        r = _parked_dot(u, w_ref[:, c0:c0 + cw], bufs[ci % 2])
        r3 = r.reshape(tm // SUBLANES, SUBLANES, cw)
        w0, w1, w2, w3 = (cw_ref[k:k + 1, c0:c0 + cw].reshape(1, 1, cw) for k in range(4))
        bias = cb_ref[:, c0:c0 + cw].reshape(1, 1, cw)
        prev = tail_ref[:, c0:c0 + cw].reshape(1, SUBLANES, cw)
        s1 = shift(r3, prev, 1)
        b_term = w1 * r3 + w0 * s1
        b_prev = w1 * prev + w0 * shift(prev, prev, 1)
        acc = bias + w3 * r3 + w2 * s1 + shift(b_term, b_prev, 2)
        o_ref[:, c0:c0 + cw] = _silu(acc).reshape(tm, cw).astype(o_ref.dtype)
        tail_ref[:, c0:c0 + cw] = r[tm - SUBLANES:tm]


def _proj_rotary_kernel(u_ref, w_ref, pos_ref, invf_ref, qdec_ref, kdec_ref,
                        q_ref, qd_ref, k_ref, kd_ref, *bufs, cw):
    tm = u_ref.shape[0]
    dk = RET_QK_HEAD_DIM
    lane = lax.broadcasted_iota(jnp.int32, (1, dk), 1)
    first_half = lane < dk // 2
    pos = pos_ref[...].astype(F32)
    ang = jnp.where(first_half, pos[0:tm // 2], pos[tm // 2:tm]) * invf_ref[...]
    cos_p, sin_p = jnp.cos(ang), jnp.sin(ang)
    cos_q, sin_q = pltpu.roll(cos_p, dk // 2, 1), pltpu.roll(sin_p, dk // 2, 1)
    cos = jnp.concatenate([jnp.where(first_half, cos_p, cos_q),
                           jnp.where(first_half, cos_q, cos_p)], axis=0)
    sin_signed = jnp.concatenate([jnp.where(first_half, -sin_p, sin_q),
                                  jnp.where(first_half, -sin_q, sin_p)], axis=0)
    k_scale = dk ** -0.5
    u = u_ref[...]
    pieces = ((0, cos, sin_signed, q_ref, qd_ref, qdec_ref),
              (RET_QK_DIM, cos * k_scale, sin_signed * k_scale, k_ref, kd_ref, kdec_ref))
    for base, cs, sn, o_ref, od_ref, dec_ref in pieces:
        for ci, c0 in enumerate(range(0, RET_QK_DIM, cw)):
            r = _parked_dot(u, w_ref[:, base + c0:base + c0 + cw], bufs[ci % 2])
            for hh in range(cw // dk):
                lo = c0 + hh * dk
                x = r[:, hh * dk:(hh + 1) * dk]
                xr = x * cs + pltpu.roll(x, dk // 2, 1) * sn
                o_ref[:, lo:lo + dk] = xr.astype(o_ref.dtype)
                xd = xr.reshape(tm // CHUNK, CHUNK, dk) * dec_ref[:, lo:lo + dk][None]
                od_ref[:, lo:lo + dk] = xd.reshape(tm, dk).astype(od_ref.dtype)


def _proj_call(u, w, *, kind, tm, cw, extra=(), n_out=1, pos=None):
    bsz, s, d = u.shape
    n = w.shape[1]
    in_specs = [_tok(tm, d), _resident(w.shape)]
    args = [u, w]
    scratch = []
    if kind == "rotary":
        body = functools.partial(_proj_rotary_kernel, cw=cw)
        in_specs.append(_tok(tm, 1))
        args.append(pos)
        out_w = n // 2
    elif kind == "conv":
        body = functools.partial(_proj_conv_kernel, cw=cw)
        scratch = [pltpu.VMEM((SUBLANES, n), F32)]
        out_w = n
    else:
        body = functools.partial(_proj_act_kernel, act=kind, cw=cw)
        out_w = n
    for arr in extra:
        in_specs.append(_resident(arr.shape))
        args.append(arr)
    scratch += [pltpu.VMEM((tm, cw), F32)] * 2
    out_shape = [jax.ShapeDtypeStruct((bsz, s, out_w), BF16)] * n_out
    out_specs = [_tok(tm, out_w)] * n_out
    res = pl.pallas_call(
        body, grid=(bsz, s // tm), in_specs=in_specs, out_specs=out_specs, out_shape=out_shape,
        scratch_shapes=scratch, compiler_params=_params(), name="proj_" + kind,
    )(*args)
    return res if n_out > 1 else res[0]


def _mixer_kernel(z_ref, g_ref, xs_ref, bc_ref, v_ref, q_ref, qd_ref, k_ref, kd_ref, dt_ref,
                  dtb_ref, alog_ref, dskip_ref, snw_ref, rnw_ref, dmask_ref, sdec_ref, expand_ref,
                  ys_ref, yr_ref, sstate, rstate, acs_s, srct_s, wsplit_s, *, cps):
    L = CHUNK
    hp = SSD_HEAD_DIM
    gw = SSD_HEADS_PER_GROUP * hp
    n_gn = SSD_N_GROUPS * SSD_D_STATE
    dk, dv = RET_QK_HEAD_DIM, RET_V_HEAD_DIM

    @pl.when(pl.program_id(1) == 0)
    def _():
        sstate[...] = jnp.zeros(sstate.shape, F32)
        rstate[...] = jnp.zeros(rstate.shape, F32)

    row = lax.broadcasted_iota(jnp.int32, (L, L), 0)
    col = lax.broadcasted_iota(jnp.int32, (L, L), 1)
    causal = row >= col
    tri = jnp.where(causal, 1.0, 0.0).astype(BF16)
    lane = lax.broadcasted_iota(jnp.int32, (1, LANES), 1)
    lo_half = lane < hp

    dt_all = jax.nn.softplus(dt_ref[...] + dtb_ref[...])
    a_all = dt_all * (-jnp.exp(alog_ref[...]))
    log_dt_all = jnp.log(dt_all)
    a1 = a_all.astype(BF16)
    r1 = a_all - a1.astype(F32)
    a2 = r1.astype(BF16)
    a3 = (r1 - a2.astype(F32)).astype(BF16)
    for ci in range(cps):
        sl = slice(ci * L, (ci + 1) * L)
        a_cs = (jnp.dot(tri, a1[sl], preferred_element_type=F32)
                + jnp.dot(tri, a2[sl], preferred_element_type=F32)
                + jnp.dot(tri, a3[sl], preferred_element_type=F32))
        a_end = a_cs[L - 1:L, :]
        acs_s[sl, :] = LOG2E * a_cs
        srct_s[ci] = (LOG2E * (a_cs - log_dt_all[sl])).T
        wts = jnp.concatenate([jnp.exp(a_cs), dt_all[sl] * jnp.exp(a_end - a_cs)], axis=0)
        w_hi = wts.astype(BF16)
        w_lo = (wts - w_hi.astype(F32)).astype(BF16)
        wsplit_s[2 * ci * L:2 * (ci + 1) * L, :] = jnp.concatenate([w_hi, w_lo], axis=1)

    def chunk_body(ci, carry):
        rows = pl.ds(pl.multiple_of(ci * L, L), L)
        a_cs2 = acs_s[rows, :]
        src_t2 = srct_s[ci]
        w_split = wsplit_s[pl.ds(pl.multiple_of(2 * ci * L, 2 * L), 2 * L), :]

        for g in range(SSD_N_GROUPS):
            gc = slice(g * gw, (g + 1) * gw)
            wide = jnp.dot(w_split, expand_ref[:, gc], preferred_element_type=F32)
            w_off = wide[0:L]
            w_state = wide[L:2 * L]
            b_g = bc_ref[rows, g * SSD_D_STATE:(g + 1) * SSD_D_STATE]
            c_g = bc_ref[rows, n_gn + g * SSD_D_STATE:n_gn + (g + 1) * SSD_D_STATE]
            cb = lax.dot_general(c_g, b_g, (((1,), (1,)), ((), ())), preferred_element_type=F32)
            st_t = sstate[g]
            y_off = jnp.dot(c_g, st_t.astype(BF16), preferred_element_type=F32) * w_off
            x_g = xs_ref[rows, gc]
            x_gf = x_g.astype(F32)
            xw = (x_gf * w_state).astype(BF16)
            sstate[g] = st_t * w_off[L - 1:L, :] + lax.dot_general(
                b_g, xw, (((0,), (0,)), ((), ())), preferred_element_type=F32)
            yg_parts = []
            for pr in range(SSD_HEADS_PER_GROUP // 2):
                h0 = g * SSD_HEADS_PER_GROUP + 2 * pr
                pc = slice(2 * pr * hp, (2 * pr + 2) * hp)
                scores = []
                for h in (h0, h0 + 1):
                    seg2 = a_cs2[:, h:h + 1] - src_t2[h:h + 1, :]
                    scores.append((cb * jnp.exp2(jnp.where(causal, seg2, -jnp.inf))).astype(BF16))
                x_pair = x_g[:, pc]
                zeros = jnp.zeros_like(x_pair)
                x_stack = jnp.concatenate([jnp.where(lo_half, x_pair, zeros),
                                           jnp.where(lo_half, zeros, x_pair)], axis=0)
                y_diag = jnp.dot(jnp.concatenate(scores, axis=1), x_stack,
                                 preferred_element_type=F32)
                c0 = g * gw + 2 * pr * hp
                y_pair = y_diag + y_off[:, pc] + x_gf[:, pc] * dskip_ref[:, c0:c0 + 2 * hp]
                yg_parts.append(y_pair * z_ref[rows, c0:c0 + 2 * hp].astype(F32))
            yg = jnp.concatenate(yg_parts, axis=1)
            ys_ref[rows, gc] = (_rms_normalize(yg) * snw_ref[:, gc]).astype(ys_ref.dtype)

        for h in range(RET_N_HEADS):
            vh = v_ref[rows, h * dv:(h + 1) * dv]
            scores = lax.dot_general(q_ref[rows, h * dk:(h + 1) * dk], k_ref[rows, h * dk:(h + 1) * dk],
                                     (((1,), (1,)), ((), ())),
                                     preferred_element_type=F32) * dmask_ref[h]
            st = rstate[h]
            y = (jnp.dot(scores.astype(BF16), vh, preferred_element_type=F32)
                 + jnp.dot(qd_ref[rows, h * dk:(h + 1) * dk], st.astype(BF16),
                           preferred_element_type=F32))
            rstate[h] = st * sdec_ref[h] + lax.dot_general(
                kd_ref[rows, h * dk:(h + 1) * dk], vh, (((0,), (0,)), ((), ())),
                preferred_element_type=F32)
            yn = _rms_normalize(y) * rnw_ref[:, h * dv:(h + 1) * dv]
            yr_ref[rows, h * dv:(h + 1) * dv] = (
                yn * g_ref[rows, h * dv:(h + 1) * dv].astype(F32)).astype(yr_ref.dtype)
        return carry

    lax.fori_loop(0, cps, chunk_body, 0, unroll=True)


def _mixer_call(zg, xbc, v, q, qd, k, kd, dt_raw, params, consts, *, cps):
    bsz, s, _ = v.shape
    lb = cps * CHUNK
    in_specs = [_tok(lb, SSD_D_INNER, 0), _tok(lb, RET_V_DIM, 1),
                _tok(lb, SSD_D_INNER, 0),
                _tok(lb, SSD_BC_DIM, SSD_D_INNER // SSD_BC_DIM),
                _tok(lb, RET_V_DIM), _tok(lb, RET_QK_DIM), _tok(lb, RET_QK_DIM),
                _tok(lb, RET_QK_DIM), _tok(lb, RET_QK_DIM), _tok(lb, LANES)]
    args = [zg, zg, xbc, xbc, v, q, qd, k, kd, dt_raw]
    for arr in list(params) + list(consts):
        in_specs.append(_resident(arr.shape))
        args.append(arr)
    scratch = [pltpu.VMEM((SSD_N_GROUPS, SSD_D_STATE, SSD_HEADS_PER_GROUP * SSD_HEAD_DIM), F32),
               pltpu.VMEM((RET_N_HEADS, RET_QK_HEAD_DIM, RET_V_HEAD_DIM), F32),
               pltpu.VMEM((lb, LANES), F32),
               pltpu.VMEM((cps, LANES, CHUNK), F32),
               pltpu.VMEM((2 * lb, 2 * LANES), BF16)]
    return pl.pallas_call(
        functools.partial(_mixer_kernel, cps=cps),
        grid=(bsz, s // lb), in_specs=in_specs,
        out_specs=[_tok(lb, SSD_D_INNER), _tok(lb, RET_V_DIM)],
        out_shape=[jax.ShapeDtypeStruct((bsz, s, SSD_D_INNER), BF16),
                   jax.ShapeDtypeStruct((bsz, s, RET_V_DIM), BF16)],
        scratch_shapes=scratch, compiler_params=_params(), name="mixer",
    )(*args)


def _merge_kernel(ys_ref, yr_ref, gate_ref, h_ref, mod_ref, wbs_ref, wbr_ref, wo_ref, o_ref):
    d = o_ref.shape[-1]
    y_ssd = jnp.dot(ys_ref[...], wbs_ref[...], preferred_element_type=F32)
    y_ret = jnp.dot(yr_ref[...], wbr_ref[...], preferred_element_type=F32)
    mix = gate_ref[:, 0:d].astype(F32) * y_ssd + gate_ref[:, d:2 * d].astype(F32) * y_ret
    out = jnp.dot(mix.astype(BF16), wo_ref[...], preferred_element_type=F32)
    o_ref[...] = h_ref[...] + mod_ref[5:6, :] * out


def _merge_call(ys, yr, gates, h, mod, wbs, wbr, wo, *, tm):
    bsz, s, d = h.shape
    in_specs = [_tok(tm, ys.shape[-1]), _tok(tm, yr.shape[-1]), _tok(tm, gates.shape[-1]),
                _tok(tm, d), pl.BlockSpec((None, N_MOD, d), lambda b, i: (b, 0, 0)),
                _resident(wbs.shape), _resident(wbr.shape), _resident(wo.shape)]
    return pl.pallas_call(
        _merge_kernel, grid=(bsz, s // tm), in_specs=in_specs, out_specs=_tok(tm, d),
        out_shape=jax.ShapeDtypeStruct((bsz, s, d), F32),
        compiler_params=_params(), name="merge",
    )(ys, yr, gates, h, mod, wbs, wbr, wo)


def _retention_constants():
    L = CHUNK
    log_gamma = jnp.log1p(-jnp.exp2(-5.0 - jnp.arange(RET_N_HEADS, dtype=F32)))
    idx = jnp.arange(L, dtype=F32)
    rel = idx[:, None] - idx[None, :]
    dmask = jnp.where(rel >= 0, jnp.exp(log_gamma[:, None, None] * jnp.maximum(rel, 0.0)), 0.0)
    q_decay = jnp.exp(log_gamma[None, :] * (idx + 1.0)[:, None])
    k_decay = jnp.exp(log_gamma[None, :] * (L - 1.0 - idx)[:, None])
    qdec = jnp.repeat(q_decay, RET_QK_HEAD_DIM, axis=1)
    kdec = jnp.repeat(k_decay, RET_QK_HEAD_DIM, axis=1)
    sdec = jnp.broadcast_to(jnp.exp(log_gamma * L)[:, None, None], (RET_N_HEADS, 1, RET_V_HEAD_DIM))
    half = RET_QK_HEAD_DIM // 2
    inv_freq = ROPE_BASE ** (-jnp.arange(half, dtype=F32) / half)
    invf = jnp.concatenate([inv_freq, inv_freq]).reshape(1, RET_QK_HEAD_DIM)
    return invf, qdec, kdec, dmask, sdec


def _pad_lanes(v):
    return jnp.pad(v.reshape(1, -1), ((0, 0), (0, LANES - v.shape[-1])))


def kernel(x, c, positions, ada_w, ada_b, norm_ffn1_w, ffn1_w_in, ffn1_w_out, norm_mix_w, mix_w_in, mix_gate_b, ssd_conv_w, ssd_conv_b, ssd_dt_bias, ssd_a_log, ssd_d, ssd_norm_w, ret_norm_w, w_br_ssd, w_br_ret, mix_w_out, norm_ffn2_w, ffn2_w_in, ffn2_w_out, norm_final_w):
    bsz, s, d = x.shape
    assert s % CHUNK == 0 and d == RET_QK_DIM and ada_w.shape[0] == 1
    tm_ffn = min(512, s)
    tm_merge = min(512, s)
    tm_proj = min(1024, s)
    cps = min(4, s // CHUNK)
    invf, qdec, kdec, dmask, sdec = _retention_constants()
    pos = positions.reshape(bsz, s, 1)
    l = 0

    mod = _ada_call(c, ada_w[l], ada_b[l]).reshape(bsz, N_MOD, d)

    w = mix_w_in[l].astype(BF16)
    o_z, o_xbc, o_dt = 0, 2048, 5120
    o_q, o_k, o_v, o_g, o_gates = 5152, 6176, 7200, 9248, 11296
    w_zg = jnp.concatenate([w[:, o_z:o_z + 2048], w[:, o_g:o_g + 2048]], axis=1)
    w_xbc = w[:, o_xbc:o_xbc + 3072]
    w_v = w[:, o_v:o_v + 2048]
    w_qk = w[:, o_q:o_q + 2048]
    w_gates = w[:, o_gates:o_gates + 2048]
    w_dt = jnp.pad(w[:, o_dt:o_dt + SSD_N_HEADS], ((0, 0), (0, LANES - SSD_N_HEADS)))

    h, u, dt_raw = _ffn_call(x, mod, norm_ffn1_w[l], ffn1_w_in[l].astype(BF16),
                             ffn1_w_out[l].astype(BF16), norm_mix_w[l], w_dt,
                             mode="mid", rows=(0, 1, 2), tm=tm_ffn)

    zg = _proj_call(u, w_zg, kind="silu", tm=tm_proj, cw=512)
    xbc = _proj_call(u, w_xbc, kind="conv", tm=tm_proj, cw=512,
                     extra=(ssd_conv_w[l], ssd_conv_b[l].reshape(1, -1)))
    v = _proj_call(u, w_v, kind="none", tm=min(2048, s), cw=512)
    q, qd, k, kd = _proj_call(u, w_qk, kind="rotary", tm=tm_proj, cw=512, pos=pos,
                              extra=(invf, qdec, kdec), n_out=4)
    gates = _proj_call(u, w_gates, kind="sigmoid_bias", tm=min(2048, s), cw=512,
                       extra=(mix_gate_b[l].reshape(1, -1),))

    params = (_pad_lanes(ssd_dt_bias[l]), _pad_lanes(ssd_a_log[l]),
              jnp.repeat(ssd_d[l], SSD_HEAD_DIM).reshape(1, -1),
              ssd_norm_w[l].reshape(1, -1), ret_norm_w[l].reshape(1, -1))
    head_of_lane = jnp.arange(SSD_D_INNER) // SSD_HEAD_DIM
    expand = (jnp.arange(LANES)[:, None] == head_of_lane[None, :]).astype(BF16)
    expand = jnp.concatenate([expand, expand], axis=0)
    ys, yr = _mixer_call(zg, xbc, v, q, qd, k, kd, dt_raw, params, (dmask, sdec, expand), cps=cps)

    h = _merge_call(ys, yr, gates, h, mod, w_br_ssd[l].astype(BF16), w_br_ret[l].astype(BF16),
                    mix_w_out[l].astype(BF16), tm=tm_merge)

    return _ffn_call(h, mod, norm_ffn2_w[l], ffn2_w_in[l].astype(BF16),
                     ffn2_w_out[l].astype(BF16), norm_final_w, None,
                     mode="final", rows=(6, 7, 8), tm=tm_ffn)
```

```python
import functools

import jax
import jax.numpy as jnp
from jax import lax
from jax.experimental import pallas as pl
from jax.experimental.pallas import tpu as pltpu

EPS = 1e-6
LOG2E = 1.4426950408889634
CHUNK = 128
SSD_HEAD_DIM = 64
SSD_N_HEADS = 32
SSD_N_GROUPS = 4
SSD_HEADS_PER_GROUP = 8
SSD_D_STATE = 128
SSD_D_INNER = 2048
SSD_CONV_WIDTH = 4
SSD_BC_DIM = 2 * SSD_N_GROUPS * SSD_D_STATE
RET_N_HEADS = 8
RET_QK_HEAD_DIM = 128
RET_V_HEAD_DIM = 256
RET_QK_DIM = 1024
RET_V_DIM = 2048
ROPE_BASE = 10000.0
FFN_RESIDUAL_WEIGHT = 0.5
N_MOD = 9

LANES = 128
SUBLANES = 8
VMEM_LIMIT_BYTES = 56 * 1024 * 1024

BF16 = jnp.bfloat16
F32 = jnp.float32


def _sigmoid(x):
    return 0.5 * jnp.tanh(0.5 * x) + 0.5


def _silu(x):
    hx = 0.5 * x
    return hx * (1.0 + jnp.tanh(hx))


def _rms_normalize(x):
    return x * lax.rsqrt(jnp.mean(x * x, axis=-1, keepdims=True) + EPS)


def _resident(shape):
    nd = len(shape)
    return pl.BlockSpec(shape, lambda *_: (0,) * nd, pipeline_mode=pl.Buffered(1))


def _tok(rows, width, col_block=0):
    return pl.BlockSpec((None, rows, width), lambda b, i: (b, i, col_block))


def _params(n_grid_axes=2):
    return pltpu.CompilerParams(dimension_semantics=("arbitrary",) * n_grid_axes,
                                vmem_limit_bytes=VMEM_LIMIT_BYTES)


def _ada_kernel(c_ref, w_ref, b_ref, o_ref):
    c_act = _silu(c_ref[...])
    o_ref[...] = jnp.dot(c_act.astype(BF16), w_ref[...].astype(BF16),
                         preferred_element_type=F32) + b_ref[...]


def _ada_call(c, w, b):
    bsz, d = c.shape
    n = w.shape[1]
    tn = d
    return pl.pallas_call(
        _ada_kernel,
        grid=(n // tn,),
        in_specs=[pl.BlockSpec((bsz, d), lambda j: (0, 0)),
                  pl.BlockSpec((d, tn), lambda j: (0, j)),
                  pl.BlockSpec((1, tn), lambda j: (0, j))],
        out_specs=pl.BlockSpec((bsz, tn), lambda j: (0, j)),
        out_shape=jax.ShapeDtypeStruct((bsz, n), F32),
        compiler_params=_params(1),
        name="ada_mod",
    )(c, w, b.reshape(1, n))


def _ffn_kernel(*refs, mode, rows, ffn_dim):
    if mode == "mid":
        (x_ref, mod_ref, nw_ref, win_ref, wout_ref, nw2_ref, wdt_ref,
         h_ref, u_ref, dt_ref) = refs
    else:
        x_ref, mod_ref, nw_ref, win_ref, wout_ref, nw2_ref, o_ref = refs
    sh_row, sc_row, g_row = rows
    h = x_ref[...]
    mod = mod_ref[...]
    u = _rms_normalize(h) * nw_ref[...] * (1.0 + mod[sc_row:sc_row + 1]) + mod[sh_row:sh_row + 1]
    gu = jnp.dot(u.astype(BF16), win_ref[...], preferred_element_type=F32)
    act = _silu(gu[:, :ffn_dim]) * gu[:, ffn_dim:]
    y = jnp.dot(act.astype(BF16), wout_ref[...], preferred_element_type=F32)
    h_new = h + FFN_RESIDUAL_WEIGHT * mod[g_row:g_row + 1] * y
    if mode == "mid":
        h_ref[...] = h_new
        u2 = (_rms_normalize(h_new) * nw2_ref[...] * (1.0 + mod[sc_row + 3:sc_row + 4])
              + mod[sh_row + 3:sh_row + 4]).astype(BF16)
        u_ref[...] = u2
        dt_ref[...] = jnp.dot(u2, wdt_ref[...], preferred_element_type=F32)
    else:
        o_ref[...] = _rms_normalize(h_new) * nw2_ref[...]


def _ffn_call(x, mod, nw, w_in, w_out, nw2, w_dt, *, mode, rows, tm):
    bsz, s, d = x.shape
    ffn_dim = w_out.shape[0]
    in_specs = [_tok(tm, d), pl.BlockSpec((None, N_MOD, d), lambda b, i: (b, 0, 0)),
                _resident((1, d)), _resident(w_in.shape), _resident(w_out.shape), _resident((1, d))]
    args = [x, mod, nw.reshape(1, d), w_in, w_out, nw2.reshape(1, d)]
    if mode == "mid":
        in_specs.append(_resident(w_dt.shape))
        args.append(w_dt)
        out_specs = [_tok(tm, d), _tok(tm, d), _tok(tm, LANES)]
        out_shape = [jax.ShapeDtypeStruct((bsz, s, d), F32),
                     jax.ShapeDtypeStruct((bsz, s, d), BF16),
                     jax.ShapeDtypeStruct((bsz, s, LANES), F32)]
    else:
        out_specs = _tok(tm, d)
        out_shape = jax.ShapeDtypeStruct((bsz, s, d), F32)
    return pl.pallas_call(
        functools.partial(_ffn_kernel, mode=mode, rows=rows, ffn_dim=ffn_dim),
        grid=(bsz, s // tm), in_specs=in_specs, out_specs=out_specs, out_shape=out_shape,
        compiler_params=_params(), name="ffn_" + mode,
    )(*args)


def _parked_dot(u, w_chunk, buf_ref):
    rows = pl.ds(pl.multiple_of(jnp.minimum(pl.program_id(1), 0), SUBLANES), u.shape[0])
    buf_ref[rows, :] = jnp.dot(u, w_chunk, preferred_element_type=F32)
    return buf_ref[rows, :]


def _seg_act(product, col0, n_cols, cw, act, b_ref, o_ref):
    for c0 in range(0, n_cols, cw):
        r = product(col0 + c0, cw)
        if act == "silu":
            r = _silu(r)
        elif act == "sigmoid_bias":
            r = _sigmoid(r + b_ref[:, c0:c0 + cw])
        o_ref[:, c0:c0 + cw] = r.astype(o_ref.dtype)


def _seg_conv(product, col0, n_cols, cw, cw_ref, cb_ref, o_ref, tail_ref):
    tm = o_ref.shape[0]
    assert SSD_CONV_WIDTH == 4

    @pl.when(pl.program_id(1) == 0)
    def _():
        tail_ref[...] = jnp.zeros(tail_ref.shape, F32)

    sub = lax.broadcasted_iota(jnp.int32, (1, SUBLANES, 1), 1)

    def shift(x3, first, j):
        n = x3.shape[0]
        rot = pltpu.roll(jnp.concatenate([first, x3], axis=0), j, 1)
        return jnp.where(sub < j, rot[0:n], rot[1:n + 1])

    for c0 in range(0, n_cols, cw):
        r = product(col0 + c0, cw)
        r3 = r.reshape(tm // SUBLANES, SUBLANES, cw)
        w0, w1, w2, w3 = (cw_ref[k:k + 1, c0:c0 + cw].reshape(1, 1, cw) for k in range(4))
        bias = cb_ref[:, c0:c0 + cw].reshape(1, 1, cw)
        prev = tail_ref[:, c0:c0 + cw].reshape(1, SUBLANES, cw)
        s1 = shift(r3, prev, 1)
        b_term = w1 * r3 + w0 * s1
        b_prev = w1 * prev + w0 * shift(prev, prev, 1)
        acc = bias + w3 * r3 + w2 * s1 + shift(b_term, b_prev, 2)
        o_ref[:, c0:c0 + cw] = _silu(acc).reshape(tm, cw).astype(o_ref.dtype)
        tail_ref[:, c0:c0 + cw] = r[tm - SUBLANES:tm]


def _seg_rotary(product, col0, cw, pos_ref, invf_ref, qdec_ref, kdec_ref,
                q_ref, qd_ref, k_ref, kd_ref):
    tm = q_ref.shape[0]
    dk = RET_QK_HEAD_DIM
    nb = tm // LANES
    assert tm % (2 * LANES) == 0 and LANES % nb == 0
    lane = lax.broadcasted_iota(jnp.int32, (1, dk), 1)
    first_half = lane < dk // 2
    pos_rows = pos_ref[...].astype(F32)
    pos_t = jnp.concatenate([pos_rows] * (LANES // nb), axis=0).T
    ang = jnp.concatenate(
        [jnp.where(first_half, pos_t[:, j:j + 1], pos_t[:, j + nb // 2:j + nb // 2 + 1])
         for j in range(nb // 2)], axis=0) * invf_ref[...]
    cos_p, sin_p = jnp.cos(ang), jnp.sin(ang)
    cos_q, sin_q = pltpu.roll(cos_p, dk // 2, 1), pltpu.roll(sin_p, dk // 2, 1)
    cos = jnp.concatenate([jnp.where(first_half, cos_p, cos_q),
                           jnp.where(first_half, cos_q, cos_p)], axis=0)
    sin_signed = jnp.concatenate([jnp.where(first_half, -sin_p, sin_q),
                                  jnp.where(first_half, -sin_q, sin_p)], axis=0)
    k_scale = dk ** -0.5
    pieces = ((0, cos, sin_signed, q_ref, qd_ref, qdec_ref),
              (RET_QK_DIM, cos * k_scale, sin_signed * k_scale, k_ref, kd_ref, kdec_ref))
    for base, cs, sn, o_ref, od_ref, dec_ref in pieces:
        for c0 in range(0, RET_QK_DIM, cw):
            r = product(col0 + base + c0, cw)
            for hh in range(cw // dk):
                lo = c0 + hh * dk
                x = r[:, hh * dk:(hh + 1) * dk]
                xr = x * cs + pltpu.roll(x, dk // 2, 1) * sn
                o_ref[:, lo:lo + dk] = xr.astype(o_ref.dtype)
                xd = xr.reshape(tm // CHUNK, CHUNK, dk) * dec_ref[:, lo:lo + dk][None]
                od_ref[:, lo:lo + dk] = xd.reshape(tm, dk).astype(od_ref.dtype)


N_PARK = 3

_SEG_ARITY = {"none": (0, 1, 0), "silu": (0, 1, 0), "sigmoid_bias": (1, 1, 0),
              "conv": (2, 1, 1), "rotary": (4, 4, 0)}


def _proj_kernel(u_ref, w_ref, *refs, plan, cw):
    n_in = sum(_SEG_ARITY[kind][0] for kind, _ in plan)
    n_out = sum(_SEG_ARITY[kind][1] for kind, _ in plan)
    ins, outs, scratch = list(refs[:n_in]), list(refs[n_in:n_in + n_out]), list(refs[n_in + n_out:])
    bufs = scratch[-N_PARK:]
    u = u_ref[...]
    n_products = [0]

    def product(c0, width):
        buf = bufs[n_products[0] % N_PARK]
        n_products[0] += 1
        return _parked_dot(u, w_ref[:, c0:c0 + width], buf)

    col0 = 0
    for kind, n_cols in plan:
        a_in, a_out, a_scr = _SEG_ARITY[kind]
        seg_in, ins = ins[:a_in], ins[a_in:]
        seg_out, outs = outs[:a_out], outs[a_out:]
        seg_scr, scratch = scratch[:a_scr], scratch[a_scr:]
        if kind == "conv":
            _seg_conv(product, col0, n_cols, cw, *seg_in, *seg_out, *seg_scr)
        elif kind == "rotary":
            _seg_rotary(product, col0, cw, *seg_in, *seg_out)
        else:
            _seg_act(product, col0, n_cols, cw, kind, seg_in[0] if seg_in else None, seg_out[0])
        col0 += n_cols


def _proj_call(u, segments, *, tm, cw, name):
    bsz, s, d = u.shape
    w_all = jnp.concatenate([w for _, w, _ in segments], axis=1)
    in_specs = [_tok(tm, d), _resident(w_all.shape)]
    args = [u, w_all]
    out_shape, out_specs, scratch, plan = [], [], [], []
    for kind, w, extra in segments:
        n = w.shape[1]
        plan.append((kind, n))
        for j, arr in enumerate(extra):
            tiled = kind == "rotary" and j == 0
            in_specs.append(_tok(tm // LANES, LANES) if tiled else _resident(arr.shape))
            args.append(arr)
        out_w = n // 2 if kind == "rotary" else n
        n_out = _SEG_ARITY[kind][1]
        out_shape += [jax.ShapeDtypeStruct((bsz, s, out_w), BF16)] * n_out
        out_specs += [_tok(tm, out_w)] * n_out
        if kind == "conv":
            scratch.append(pltpu.VMEM((SUBLANES, n), F32))
    scratch += [pltpu.VMEM((tm, cw), F32)] * N_PARK
    return pl.pallas_call(
        functools.partial(_proj_kernel, plan=tuple(plan), cw=cw),
        grid=(bsz, s // tm), in_specs=in_specs, out_specs=out_specs, out_shape=out_shape,
        scratch_shapes=scratch, compiler_params=_params(), name=name,
    )(*args)


def _mixer_kernel(z_ref, g_ref, xs_ref, bc_ref, v_ref, q_ref, qd_ref, k_ref, kd_ref, dt_ref,
                  dtb_ref, alog_ref, dskip_ref, snw_ref, rnw_ref, dmask_ref, sdec_ref, expand_ref,
                  ys_ref, yr_ref, sstate, rstate, acs_s, srct_s, wsplit_s, *, cps):
    L = CHUNK
    hp = SSD_HEAD_DIM
    gw = SSD_HEADS_PER_GROUP * hp
    n_gn = SSD_N_GROUPS * SSD_D_STATE
    dk, dv = RET_QK_HEAD_DIM, RET_V_HEAD_DIM

    @pl.when(pl.program_id(1) == 0)
    def _():
        sstate[...] = jnp.zeros(sstate.shape, F32)
        rstate[...] = jnp.zeros(rstate.shape, F32)

    row = lax.broadcasted_iota(jnp.int32, (L, L), 0)
    col = lax.broadcasted_iota(jnp.int32, (L, L), 1)
    causal = row >= col
    tri = jnp.where(causal, 1.0, 0.0).astype(BF16)
    lane = lax.broadcasted_iota(jnp.int32, (1, LANES), 1)
    lo_half = lane < hp

    dt_all = jax.nn.softplus(dt_ref[...] + dtb_ref[...])
    a_all = dt_all * (-jnp.exp(alog_ref[...]))
    log_dt_all = jnp.log(dt_all)
    a1 = a_all.astype(BF16)
    r1 = a_all - a1.astype(F32)
    a2 = r1.astype(BF16)
    a3 = (r1 - a2.astype(F32)).astype(BF16)
    for ci in range(cps):
        sl = slice(ci * L, (ci + 1) * L)
        a_cs = (jnp.dot(tri, a1[sl], preferred_element_type=F32)
                + jnp.dot(tri, a2[sl], preferred_element_type=F32)
                + jnp.dot(tri, a3[sl], preferred_element_type=F32))
        a_end = a_cs[L - 1:L, :]
        acs_s[sl, :] = LOG2E * a_cs
        srct_s[ci] = (LOG2E * (a_cs - log_dt_all[sl])).T
        wts = jnp.concatenate([jnp.exp(a_cs), dt_all[sl] * jnp.exp(a_end - a_cs)], axis=0)
        w_hi = wts.astype(BF16)
        w_lo = (wts - w_hi.astype(F32)).astype(BF16)
        wsplit_s[2 * ci * L:2 * (ci + 1) * L, :] = jnp.concatenate([w_hi, w_lo], axis=1)

    def chunk_body(ci, carry):
        rows = pl.ds(pl.multiple_of(ci * L, L), L)
        a_cs2 = acs_s[rows, :]
        src_t2 = srct_s[ci]
        w_split = wsplit_s[pl.ds(pl.multiple_of(2 * ci * L, 2 * L), 2 * L), :]

        for g in range(SSD_N_GROUPS):
            gc = slice(g * gw, (g + 1) * gw)
            wide = jnp.dot(w_split, expand_ref[:, gc], preferred_element_type=F32)
            w_off = wide[0:L]
            w_state = wide[L:2 * L]
            b_g = bc_ref[rows, g * SSD_D_STATE:(g + 1) * SSD_D_STATE]
            c_g = bc_ref[rows, n_gn + g * SSD_D_STATE:n_gn + (g + 1) * SSD_D_STATE]
            cb = lax.dot_general(c_g, b_g, (((1,), (1,)), ((), ())), preferred_element_type=F32)
            st_t = sstate[g]
            y_off = jnp.dot(c_g, st_t.astype(BF16), preferred_element_type=F32) * w_off
            x_g = xs_ref[rows, gc]
            x_gf = x_g.astype(F32)
            xw = (x_gf * w_state).astype(BF16)
            sstate[g] = st_t * w_off[L - 1:L, :] + lax.dot_general(
                b_g, xw, (((0,), (0,)), ((), ())), preferred_element_type=F32)
            yg_parts = []
            for pr in range(SSD_HEADS_PER_GROUP // 2):
                h0 = g * SSD_HEADS_PER_GROUP + 2 * pr
                pc = slice(2 * pr * hp, (2 * pr + 2) * hp)
                scores = []
                for h in (h0, h0 + 1):
                    seg2 = a_cs2[:, h:h + 1] - src_t2[h:h + 1, :]
                    scores.append((cb * jnp.exp2(jnp.where(causal, seg2, -jnp.inf))).astype(BF16))
                x_pair = x_g[:, pc]
                zeros = jnp.zeros_like(x_pair)
                x_stack = jnp.concatenate([jnp.where(lo_half, x_pair, zeros),
                                           jnp.where(lo_half, zeros, x_pair)], axis=0)
                y_diag = jnp.dot(jnp.concatenate(scores, axis=1), x_stack,
                                 preferred_element_type=F32)
                c0 = g * gw + 2 * pr * hp
                y_pair = y_diag + y_off[:, pc] + x_gf[:, pc] * dskip_ref[:, c0:c0 + 2 * hp]
                yg_parts.append(y_pair * z_ref[rows, c0:c0 + 2 * hp].astype(F32))
            yg = jnp.concatenate(yg_parts, axis=1)
            ys_ref[rows, gc] = (_rms_normalize(yg) * snw_ref[:, gc]).astype(ys_ref.dtype)

        for h in range(RET_N_HEADS):
            vh = v_ref[rows, h * dv:(h + 1) * dv]
            scores = lax.dot_general(q_ref[rows, h * dk:(h + 1) * dk], k_ref[rows, h * dk:(h + 1) * dk],
                                     (((1,), (1,)), ((), ())),
                                     preferred_element_type=F32) * dmask_ref[h]
            st = rstate[h]
            y = (jnp.dot(scores.astype(BF16), vh, preferred_element_type=F32)
                 + jnp.dot(qd_ref[rows, h * dk:(h + 1) * dk], st.astype(BF16),
                           preferred_element_type=F32))
            rstate[h] = st * sdec_ref[h] + lax.dot_general(
                kd_ref[rows, h * dk:(h + 1) * dk], vh, (((0,), (0,)), ((), ())),
                preferred_element_type=F32)
            yn = _rms_normalize(y) * rnw_ref[:, h * dv:(h + 1) * dv]
            yr_ref[rows, h * dv:(h + 1) * dv] = (
                yn * g_ref[rows, h * dv:(h + 1) * dv].astype(F32)).astype(yr_ref.dtype)
        return carry

    lax.fori_loop(0, cps, chunk_body, 0, unroll=True)


def _mixer_call(z, g, xbc, v, q, qd, k, kd, dt_raw, params, consts, *, cps):
    bsz, s, _ = v.shape
    lb = cps * CHUNK
    in_specs = [_tok(lb, SSD_D_INNER), _tok(lb, RET_V_DIM),
                _tok(lb, SSD_D_INNER, 0),
                _tok(lb, SSD_BC_DIM, SSD_D_INNER // SSD_BC_DIM),
                _tok(lb, RET_V_DIM), _tok(lb, RET_QK_DIM), _tok(lb, RET_QK_DIM),
                _tok(lb, RET_QK_DIM), _tok(lb, RET_QK_DIM), _tok(lb, LANES)]
    args = [z, g, xbc, xbc, v, q, qd, k, kd, dt_raw]
    for arr in list(params) + list(consts):
        in_specs.append(_resident(arr.shape))
        args.append(arr)
    scratch = [pltpu.VMEM((SSD_N_GROUPS, SSD_D_STATE, SSD_HEADS_PER_GROUP * SSD_HEAD_DIM), F32),
               pltpu.VMEM((RET_N_HEADS, RET_QK_HEAD_DIM, RET_V_HEAD_DIM), F32),
               pltpu.VMEM((lb, LANES), F32),
               pltpu.VMEM((cps, LANES, CHUNK), F32),
               pltpu.VMEM((2 * lb, 2 * LANES), BF16)]
    return pl.pallas_call(
        functools.partial(_mixer_kernel, cps=cps),
        grid=(bsz, s // lb), in_specs=in_specs,
        out_specs=[_tok(lb, SSD_D_INNER), _tok(lb, RET_V_DIM)],
        out_shape=[jax.ShapeDtypeStruct((bsz, s, SSD_D_INNER), BF16),
                   jax.ShapeDtypeStruct((bsz, s, RET_V_DIM), BF16)],
        scratch_shapes=scratch, compiler_params=_params(), name="mixer",
    )(*args)


def _merge_kernel(ys_ref, yr_ref, gate_ref, h_ref, mod_ref, wbs_ref, wbr_ref, wo_ref, o_ref):
    d = o_ref.shape[-1]
    y_ssd = jnp.dot(ys_ref[...], wbs_ref[...], preferred_element_type=F32)
    y_ret = jnp.dot(yr_ref[...], wbr_ref[...], preferred_element_type=F32)
    mix = gate_ref[:, 0:d].astype(F32) * y_ssd + gate_ref[:, d:2 * d].astype(F32) * y_ret
    out = jnp.dot(mix.astype(BF16), wo_ref[...], preferred_element_type=F32)
    o_ref[...] = h_ref[...] + mod_ref[5:6, :] * out


def _merge_call(ys, yr, gates, h, mod, wbs, wbr, wo, *, tm):
    bsz, s, d = h.shape
    in_specs = [_tok(tm, ys.shape[-1]), _tok(tm, yr.shape[-1]), _tok(tm, gates.shape[-1]),
                _tok(tm, d), pl.BlockSpec((None, N_MOD, d), lambda b, i: (b, 0, 0)),
                _resident(wbs.shape), _resident(wbr.shape), _resident(wo.shape)]
    return pl.pallas_call(
        _merge_kernel, grid=(bsz, s // tm), in_specs=in_specs, out_specs=_tok(tm, d),
        out_shape=jax.ShapeDtypeStruct((bsz, s, d), F32),
        compiler_params=_params(), name="merge",
    )(ys, yr, gates, h, mod, wbs, wbr, wo)


def _retention_constants():
    L = CHUNK
    log_gamma = jnp.log1p(-jnp.exp2(-5.0 - jnp.arange(RET_N_HEADS, dtype=F32)))
    idx = jnp.arange(L, dtype=F32)
    rel = idx[:, None] - idx[None, :]
    dmask = jnp.where(rel >= 0, jnp.exp(log_gamma[:, None, None] * jnp.maximum(rel, 0.0)), 0.0)
    q_decay = jnp.exp(log_gamma[None, :] * (idx + 1.0)[:, None])
    k_decay = jnp.exp(log_gamma[None, :] * (L - 1.0 - idx)[:, None])
    qdec = jnp.repeat(q_decay, RET_QK_HEAD_DIM, axis=1)
    kdec = jnp.repeat(k_decay, RET_QK_HEAD_DIM, axis=1)
    sdec = jnp.broadcast_to(jnp.exp(log_gamma * L)[:, None, None], (RET_N_HEADS, 1, RET_V_HEAD_DIM))
    half = RET_QK_HEAD_DIM // 2
    inv_freq = ROPE_BASE ** (-jnp.arange(half, dtype=F32) / half)
    invf = jnp.concatenate([inv_freq, inv_freq]).reshape(1, RET_QK_HEAD_DIM)
    return invf, qdec, kdec, dmask, sdec


def _pad_lanes(v):
    return jnp.pad(v.reshape(1, -1), ((0, 0), (0, LANES - v.shape[-1])))


def kernel(x, c, positions, ada_w, ada_b, norm_ffn1_w, ffn1_w_in, ffn1_w_out, norm_mix_w, mix_w_in, mix_gate_b, ssd_conv_w, ssd_conv_b, ssd_dt_bias, ssd_a_log, ssd_d, ssd_norm_w, ret_norm_w, w_br_ssd, w_br_ret, mix_w_out, norm_ffn2_w, ffn2_w_in, ffn2_w_out, norm_final_w):
    bsz, s, d = x.shape
    assert s % CHUNK == 0 and d == RET_QK_DIM and ada_w.shape[0] == 1
    tm_ffn = min(512, s)
    tm_merge = min(512, s)
    tm_proj = min(1024, s)
    cps = min(4, s // CHUNK)
    invf, qdec, kdec, dmask, sdec = _retention_constants()
    pos = positions.reshape(bsz, s // LANES, LANES)
    l = 0

    mod = _ada_call(c, ada_w[l], ada_b[l]).reshape(bsz, N_MOD, d)

    w = mix_w_in[l].astype(BF16)
    o_z, o_xbc, o_dt = 0, 2048, 5120
    o_q, o_k, o_v, o_g, o_gates = 5152, 6176, 7200, 9248, 11296
    w_z = w[:, o_z:o_z + 2048]
    w_g = w[:, o_g:o_g + 2048]
    w_xbc = w[:, o_xbc:o_xbc + 3072]
    w_v = w[:, o_v:o_v + 2048]
    w_qk = w[:, o_q:o_q + 2048]
    w_gates = w[:, o_gates:o_gates + 2048]
    w_dt = jnp.pad(w[:, o_dt:o_dt + SSD_N_HEADS], ((0, 0), (0, LANES - SSD_N_HEADS)))

    h, u, dt_raw = _ffn_call(x, mod, norm_ffn1_w[l], ffn1_w_in[l].astype(BF16),
                             ffn1_w_out[l].astype(BF16), norm_mix_w[l], w_dt,
                             mode="mid", rows=(0, 1, 2), tm=tm_ffn)

    xbc, v = _proj_call(
        u, (("conv", w_xbc, (ssd_conv_w[l], ssd_conv_b[l].reshape(1, -1))), ("none", w_v, ())),
        tm=tm_proj, cw=512, name="proj_xbc_v")
    q, qd, k, kd, z = _proj_call(
        u, (("rotary", w_qk, (pos, invf, qdec, kdec)), ("silu", w_z, ())),
        tm=tm_proj, cw=512, name="proj_qk_z")
    g, gates = _proj_call(
        u, (("silu", w_g, ()), ("sigmoid_bias", w_gates, (mix_gate_b[l].reshape(1, -1),))),
        tm=tm_proj, cw=512, name="proj_g_gates")

    params = (_pad_lanes(ssd_dt_bias[l]), _pad_lanes(ssd_a_log[l]),
              jnp.repeat(ssd_d[l], SSD_HEAD_DIM).reshape(1, -1),
              ssd_norm_w[l].reshape(1, -1), ret_norm_w[l].reshape(1, -1))
    head_of_lane = jnp.arange(SSD_D_INNER) // SSD_HEAD_DIM
    expand = (jnp.arange(LANES)[:, None] == head_of_lane[None, :]).astype(BF16)
    expand = jnp.concatenate([expand, expand], axis=0)
    ys, yr = _mixer_call(z, g, xbc, v, q, qd, k, kd, dt_raw, params, (dmask, sdec, expand), cps=cps)

    h = _merge_call(ys, yr, gates, h, mod, w_br_ssd[l].astype(BF16), w_br_ret[l].astype(BF16),
                    mix_w_out[l].astype(BF16), tm=tm_merge)

    return _ffn_call(h, mod, norm_ffn2_w[l], ffn2_w_in[l].astype(BF16),
                     ffn2_w_out[l].astype(BF16), norm_final_w, None,
                     mode="final", rows=(6, 7, 8), tm=tm_ffn)
```

```python
import functools

import jax
import jax.numpy as jnp
from jax import lax
from jax.experimental import pallas as pl
from jax.experimental.pallas import tpu as pltpu

EPS = 1e-6
LOG2E = 1.4426950408889634
CHUNK = 128
SSD_HEAD_DIM = 64
SSD_N_HEADS = 32
SSD_N_GROUPS = 4
SSD_HEADS_PER_GROUP = 8
SSD_D_STATE = 128
SSD_D_INNER = 2048
SSD_CONV_WIDTH = 4
SSD_BC_DIM = 2 * SSD_N_GROUPS * SSD_D_STATE
RET_N_HEADS = 8
RET_QK_HEAD_DIM = 128
RET_V_HEAD_DIM = 256
RET_QK_DIM = 1024
RET_V_DIM = 2048
ROPE_BASE = 10000.0
FFN_RESIDUAL_WEIGHT = 0.5
N_MOD = 9

LANES = 128
SUBLANES = 8
VMEM_LIMIT_BYTES = 56 * 1024 * 1024

BF16 = jnp.bfloat16
F32 = jnp.float32


def _sigmoid(x):
    return 0.5 * jnp.tanh(0.5 * x) + 0.5


def _silu(x):
    hx = 0.5 * x
    return hx * (1.0 + jnp.tanh(hx))


def _rms_normalize(x):
    return x * lax.rsqrt(jnp.mean(x * x, axis=-1, keepdims=True) + EPS)


def _resident(shape):
    nd = len(shape)
    return pl.BlockSpec(shape, lambda *_: (0,) * nd, pipeline_mode=pl.Buffered(1))


def _tok(rows, width, col_block=0):
    return pl.BlockSpec((None, rows, width), lambda b, i: (b, i, col_block))


def _params(n_grid_axes=2):
    return pltpu.CompilerParams(dimension_semantics=("arbitrary",) * n_grid_axes,
                                vmem_limit_bytes=VMEM_LIMIT_BYTES)


def _ada_kernel(c_ref, w_ref, b_ref, o_ref):
    c_act = _silu(c_ref[...])
    o_ref[...] = jnp.dot(c_act.astype(BF16), w_ref[...].astype(BF16),
                         preferred_element_type=F32) + b_ref[...]


def _ada_call(c, w, b):
    bsz, d = c.shape
    n = w.shape[1]
    tn = d
    return pl.pallas_call(
        _ada_kernel,
        grid=(n // tn,),
        in_specs=[pl.BlockSpec((bsz, d), lambda j: (0, 0)),
                  pl.BlockSpec((d, tn), lambda j: (0, j)),
                  pl.BlockSpec((1, tn), lambda j: (0, j))],
        out_specs=pl.BlockSpec((bsz, tn), lambda j: (0, j)),
        out_shape=jax.ShapeDtypeStruct((bsz, n), F32),
        compiler_params=_params(1),
        name="ada_mod",
    )(c, w, b.reshape(1, n))


def _ffn_kernel(*refs, mode, rows, ffn_dim):
    if mode == "mid":
        (x_ref, mod_ref, nw_ref, win_ref, wout_ref, nw2_ref, wdt_ref,
         h_ref, u_ref, dt_ref) = refs
    else:
        x_ref, mod_ref, nw_ref, win_ref, wout_ref, nw2_ref, o_ref = refs
    sh_row, sc_row, g_row = rows
    mod = mod_ref[...]
    tm = x_ref.shape[0]
    for r0 in range(0, tm, tm // 2):
        sl = slice(r0, r0 + tm // 2)
        h = x_ref[sl, :]
        u = _rms_normalize(h) * nw_ref[...] * (1.0 + mod[sc_row:sc_row + 1]) + mod[sh_row:sh_row + 1]
        gu = jnp.dot(u.astype(BF16), win_ref[...], preferred_element_type=F32)
        act = _silu(gu[:, :ffn_dim]) * gu[:, ffn_dim:]
        y = jnp.dot(act.astype(BF16), wout_ref[...], preferred_element_type=F32)
        h_new = h + FFN_RESIDUAL_WEIGHT * mod[g_row:g_row + 1] * y
        if mode == "mid":
            h_ref[sl, :] = h_new
            u2 = (_rms_normalize(h_new) * nw2_ref[...] * (1.0 + mod[sc_row + 3:sc_row + 4])
                  + mod[sh_row + 3:sh_row + 4]).astype(BF16)
            u_ref[sl, :] = u2
            dt_ref[sl, :] = jnp.dot(u2, wdt_ref[...], preferred_element_type=F32)
        else:
            o_ref[sl, :] = _rms_normalize(h_new) * nw2_ref[...]


def _ffn_call(x, mod, nw, w_in, w_out, nw2, w_dt, *, mode, rows, tm):
    bsz, s, d = x.shape
    ffn_dim = w_out.shape[0]
    in_specs = [_tok(tm, d), pl.BlockSpec((None, N_MOD, d), lambda b, i: (b, 0, 0)),
                _resident((1, d)), _resident(w_in.shape), _resident(w_out.shape), _resident((1, d))]
    args = [x, mod, nw.reshape(1, d), w_in, w_out, nw2.reshape(1, d)]
    if mode == "mid":
        in_specs.append(_resident(w_dt.shape))
        args.append(w_dt)
        out_specs = [_tok(tm, d), _tok(tm, d), _tok(tm, LANES)]
        out_shape = [jax.ShapeDtypeStruct((bsz, s, d), F32),
                     jax.ShapeDtypeStruct((bsz, s, d), BF16),
                     jax.ShapeDtypeStruct((bsz, s, LANES), F32)]
    else:
        out_specs = _tok(tm, d)
        out_shape = jax.ShapeDtypeStruct((bsz, s, d), F32)
    return pl.pallas_call(
        functools.partial(_ffn_kernel, mode=mode, rows=rows, ffn_dim=ffn_dim),
        grid=(bsz, s // tm), in_specs=in_specs, out_specs=out_specs, out_shape=out_shape,
        compiler_params=_params(), name="ffn_" + mode,
    )(*args)


def _parked_dot(u, w_chunk, buf_ref):
    rows = pl.ds(pl.multiple_of(jnp.minimum(pl.program_id(1), 0), SUBLANES), u.shape[0])
    buf_ref[rows, :] = jnp.dot(u, w_chunk, preferred_element_type=F32)
    return buf_ref[rows, :]


def _seg_act(product, col0, n_cols, cw, act, b_ref, o_ref):
    for c0 in range(0, n_cols, cw):
        r = product(col0 + c0, cw)
        if act == "silu":
            r = _silu(r)
        elif act == "sigmoid_bias":
            r = _sigmoid(r + b_ref[:, c0:c0 + cw])
        o_ref[:, c0:c0 + cw] = r.astype(o_ref.dtype)


def _seg_conv(product, col0, n_cols, cw, cw_ref, cb_ref, o_ref, tail_ref):
    tm = o_ref.shape[0]
    assert SSD_CONV_WIDTH == 4

    @pl.when(pl.program_id(1) == 0)
    def _():
        tail_ref[...] = jnp.zeros(tail_ref.shape, F32)

    sub = lax.broadcasted_iota(jnp.int32, (1, SUBLANES, 1), 1)

    def shift(x3, first, j):
        n = x3.shape[0]
        rot = pltpu.roll(jnp.concatenate([first, x3], axis=0), j, 1)
        return jnp.where(sub < j, rot[0:n], rot[1:n + 1])

    for c0 in range(0, n_cols, cw):
        r = product(col0 + c0, cw)
        r3 = r.reshape(tm // SUBLANES, SUBLANES, cw)
        w0, w1, w2, w3 = (cw_ref[k:k + 1, c0:c0 + cw].reshape(1, 1, cw) for k in range(4))
        bias = cb_ref[:, c0:c0 + cw].reshape(1, 1, cw)
        prev = tail_ref[:, c0:c0 + cw].reshape(1, SUBLANES, cw)
        s1 = shift(r3, prev, 1)
        b_term = w1 * r3 + w0 * s1
        b_prev = w1 * prev + w0 * shift(prev, prev, 1)
        acc = bias + w3 * r3 + w2 * s1 + shift(b_term, b_prev, 2)
        o_ref[:, c0:c0 + cw] = _silu(acc).reshape(tm, cw).astype(o_ref.dtype)
        tail_ref[:, c0:c0 + cw] = r[tm - SUBLANES:tm]


def _seg_rotary(product, col0, cw, pos_ref, invf_ref, q_ref, k_ref):
    tm = q_ref.shape[0]
    dk = RET_QK_HEAD_DIM
    nb = tm // LANES
    assert tm % (2 * LANES) == 0 and LANES % nb == 0
    lane = lax.broadcasted_iota(jnp.int32, (1, dk), 1)
    first_half = lane < dk // 2
    pos_rows = pos_ref[...].astype(F32)
    pos_t = jnp.concatenate([pos_rows] * (LANES // nb), axis=0).T
    ang = jnp.concatenate(
        [jnp.where(first_half, pos_t[:, j:j + 1], pos_t[:, j + nb // 2:j + nb // 2 + 1])
         for j in range(nb // 2)], axis=0) * invf_ref[...]
    cos_p, sin_p = jnp.cos(ang), jnp.sin(ang)
    cos_q, sin_q = pltpu.roll(cos_p, dk // 2, 1), pltpu.roll(sin_p, dk // 2, 1)
    cos = jnp.concatenate([jnp.where(first_half, cos_p, cos_q),
                           jnp.where(first_half, cos_q, cos_p)], axis=0)
    sin_signed = jnp.concatenate([jnp.where(first_half, -sin_p, sin_q),
                                  jnp.where(first_half, -sin_q, sin_p)], axis=0)
    k_scale = dk ** -0.5
    pieces = ((0, cos, sin_signed, q_ref),
              (RET_QK_DIM, cos * k_scale, sin_signed * k_scale, k_ref))
    for base, cs, sn, o_ref in pieces:
        for c0 in range(0, RET_QK_DIM, cw):
            r = product(col0 + base + c0, cw)
            for hh in range(cw // dk):
                lo = c0 + hh * dk
                x = r[:, hh * dk:(hh + 1) * dk]
                xr = x * cs + pltpu.roll(x, dk // 2, 1) * sn
                o_ref[:, lo:lo + dk] = xr.astype(o_ref.dtype)


N_PARK = 3

_SEG_ARITY = {"none": (0, 1, 0), "silu": (0, 1, 0), "sigmoid_bias": (1, 1, 0),
              "conv": (2, 1, 1), "rotary": (2, 2, 0)}


def _proj_kernel(u_ref, w_ref, *refs, plan, cw):
    n_in = sum(_SEG_ARITY[kind][0] for kind, _ in plan)
    n_out = sum(_SEG_ARITY[kind][1] for kind, _ in plan)
    ins, outs, scratch = list(refs[:n_in]), list(refs[n_in:n_in + n_out]), list(refs[n_in + n_out:])
    bufs = scratch[-N_PARK:]
    u = u_ref[...]
    n_products = [0]

    def product(c0, width):
        buf = bufs[n_products[0] % N_PARK]
        n_products[0] += 1
        return _parked_dot(u, w_ref[:, c0:c0 + width], buf)

    col0 = 0
    for kind, n_cols in plan:
        a_in, a_out, a_scr = _SEG_ARITY[kind]
        seg_in, ins = ins[:a_in], ins[a_in:]
        seg_out, outs = outs[:a_out], outs[a_out:]
        seg_scr, scratch = scratch[:a_scr], scratch[a_scr:]
        if kind == "conv":
            _seg_conv(product, col0, n_cols, cw, *seg_in, *seg_out, *seg_scr)
        elif kind == "rotary":
            _seg_rotary(product, col0, cw, *seg_in, *seg_out)
        else:
            _seg_act(product, col0, n_cols, cw, kind, seg_in[0] if seg_in else None, seg_out[0])
        col0 += n_cols


def _proj_call(u, segments, *, tm, cw, name):
    bsz, s, d = u.shape
    w_all = jnp.concatenate([w for _, w, _ in segments], axis=1)
    in_specs = [_tok(tm, d), _resident(w_all.shape)]
    args = [u, w_all]
    out_shape, out_specs, scratch, plan = [], [], [], []
    for kind, w, extra in segments:
        n = w.shape[1]
        plan.append((kind, n))
        for j, arr in enumerate(extra):
            tiled = kind == "rotary" and j == 0
            in_specs.append(_tok(tm // LANES, LANES) if tiled else _resident(arr.shape))
            args.append(arr)
        out_w = n // 2 if kind == "rotary" else n
        n_out = _SEG_ARITY[kind][1]
        out_shape += [jax.ShapeDtypeStruct((bsz, s, out_w), BF16)] * n_out
        out_specs += [_tok(tm, out_w)] * n_out
        if kind == "conv":
            scratch.append(pltpu.VMEM((SUBLANES, n), F32))
    scratch += [pltpu.VMEM((tm, cw), F32)] * N_PARK
    return pl.pallas_call(
        functools.partial(_proj_kernel, plan=tuple(plan), cw=cw),
        grid=(bsz, s // tm), in_specs=in_specs, out_specs=out_specs, out_shape=out_shape,
        scratch_shapes=scratch, compiler_params=_params(), name=name,
    )(*args)


def _mixer_kernel(z_ref, g_ref, xs_ref, bc_ref, v_ref, q_ref, k_ref, dt_ref,
                  dtb_ref, alog_ref, dskip_ref, snw_ref, rnw_ref, dmask_ref, sdec_ref, expand_ref,
                  qdec_ref, kdec_ref,
                  ys_ref, yr_ref, sstate, rstate, acs_s, srct_s, wsplit_s, *, cps):
    L = CHUNK
    hp = SSD_HEAD_DIM
    gw = SSD_HEADS_PER_GROUP * hp
    n_gn = SSD_N_GROUPS * SSD_D_STATE
    dk, dv = RET_QK_HEAD_DIM, RET_V_HEAD_DIM

    @pl.when(pl.program_id(1) == 0)
    def _():
        sstate[...] = jnp.zeros(sstate.shape, F32)
        rstate[...] = jnp.zeros(rstate.shape, F32)

    row = lax.broadcasted_iota(jnp.int32, (L, L), 0)
    col = lax.broadcasted_iota(jnp.int32, (L, L), 1)
    causal = row >= col
    tri = jnp.where(causal, 1.0, 0.0).astype(BF16)
    lane = lax.broadcasted_iota(jnp.int32, (1, LANES), 1)
    lo_half = lane < hp

    dt_all = jax.nn.softplus(dt_ref[...] + dtb_ref[...])
    a_all = dt_all * (-jnp.exp(alog_ref[...]))
    log_dt_all = jnp.log(dt_all)
    a1 = a_all.astype(BF16)
    r1 = a_all - a1.astype(F32)
    a2 = r1.astype(BF16)
    a3 = (r1 - a2.astype(F32)).astype(BF16)
    for ci in range(cps):
        sl = slice(ci * L, (ci + 1) * L)
        a_cs = (jnp.dot(tri, a1[sl], preferred_element_type=F32)
                + jnp.dot(tri, a2[sl], preferred_element_type=F32)
                + jnp.dot(tri, a3[sl], preferred_element_type=F32))
        a_end = a_cs[L - 1:L, :]
        acs_s[sl, :] = LOG2E * a_cs
        srct_s[ci] = (LOG2E * (a_cs - log_dt_all[sl])).T
        wts = jnp.concatenate([jnp.exp(a_cs), dt_all[sl] * jnp.exp(a_end - a_cs)], axis=0)
        w_hi = wts.astype(BF16)
        w_lo = (wts - w_hi.astype(F32)).astype(BF16)
        wsplit_s[2 * ci * L:2 * (ci + 1) * L, :] = jnp.concatenate([w_hi, w_lo], axis=1)

    def chunk_body(ci, carry):
        rows = pl.ds(pl.multiple_of(ci * L, L), L)
        a_cs2 = acs_s[rows, :]
        src_t2 = srct_s[ci]
        w_split = wsplit_s[pl.ds(pl.multiple_of(2 * ci * L, 2 * L), 2 * L), :]

        for g in range(SSD_N_GROUPS):
            gc = slice(g * gw, (g + 1) * gw)
            wide = jnp.dot(w_split, expand_ref[:, gc], preferred_element_type=F32)
            w_off = wide[0:L]
            w_state = wide[L:2 * L]
            b_g = bc_ref[rows, g * SSD_D_STATE:(g + 1) * SSD_D_STATE]
            c_g = bc_ref[rows, n_gn + g * SSD_D_STATE:n_gn + (g + 1) * SSD_D_STATE]
            cb = lax.dot_general(c_g, b_g, (((1,), (1,)), ((), ())), preferred_element_type=F32)
            st_t = sstate[g]
            y_off = jnp.dot(c_g, st_t.astype(BF16), preferred_element_type=F32) * w_off
            x_g = xs_ref[rows, gc]
            x_gf = x_g.astype(F32)
            xw = (x_gf * w_state).astype(BF16)
            sstate[g] = st_t * w_off[L - 1:L, :] + lax.dot_general(
                b_g, xw, (((0,), (0,)), ((), ())), preferred_element_type=F32)
            yg_parts = []
            for pr in range(SSD_HEADS_PER_GROUP // 2):
                h0 = g * SSD_HEADS_PER_GROUP + 2 * pr
                pc = slice(2 * pr * hp, (2 * pr + 2) * hp)
                scores = []
                for h in (h0, h0 + 1):
                    seg2 = a_cs2[:, h:h + 1] - src_t2[h:h + 1, :]
                    scores.append((cb * jnp.exp2(jnp.where(causal, seg2, -jnp.inf))).astype(BF16))
                x_pair = x_g[:, pc]
                zeros = jnp.zeros_like(x_pair)
                x_stack = jnp.concatenate([jnp.where(lo_half, x_pair, zeros),
                                           jnp.where(lo_half, zeros, x_pair)], axis=0)
                y_diag = jnp.dot(jnp.concatenate(scores, axis=1), x_stack,
                                 preferred_element_type=F32)
                c0 = g * gw + 2 * pr * hp
                y_pair = y_diag + y_off[:, pc] + x_gf[:, pc] * dskip_ref[:, c0:c0 + 2 * hp]
                yg_parts.append(y_pair * z_ref[rows, c0:c0 + 2 * hp].astype(F32))
            yg = jnp.concatenate(yg_parts, axis=1)
            ys_ref[rows, gc] = (_rms_normalize(yg) * snw_ref[:, gc]).astype(ys_ref.dtype)

        for h in range(RET_N_HEADS):
            hc = slice(h * dk, (h + 1) * dk)
            vh = v_ref[rows, h * dv:(h + 1) * dv]
            qh, kh = q_ref[rows, hc], k_ref[rows, hc]
            scores = lax.dot_general(qh, kh, (((1,), (1,)), ((), ())),
                                     preferred_element_type=F32) * dmask_ref[h]
            st = rstate[h]
            y = (jnp.dot(scores.astype(BF16), vh, preferred_element_type=F32)
                 + jnp.dot(qh * qdec_ref[:, hc], st.astype(BF16), preferred_element_type=F32))
            rstate[h] = st * sdec_ref[h] + lax.dot_general(
                kh * kdec_ref[:, hc], vh, (((0,), (0,)), ((), ())), preferred_element_type=F32)
            yn = _rms_normalize(y) * rnw_ref[:, h * dv:(h + 1) * dv]
            yr_ref[rows, h * dv:(h + 1) * dv] = (
                yn * g_ref[rows, h * dv:(h + 1) * dv].astype(F32)).astype(yr_ref.dtype)
        return carry

    lax.fori_loop(0, cps, chunk_body, 0, unroll=True)


def _mixer_call(z, g, xbc, v, q, k, dt_raw, params, consts, *, cps):
    bsz, s, _ = v.shape
    lb = cps * CHUNK
    in_specs = [_tok(lb, SSD_D_INNER), _tok(lb, RET_V_DIM),
                _tok(lb, SSD_D_INNER, 0),
                _tok(lb, SSD_BC_DIM, SSD_D_INNER // SSD_BC_DIM),
                _tok(lb, RET_V_DIM), _tok(lb, RET_QK_DIM), _tok(lb, RET_QK_DIM), _tok(lb, LANES)]
    args = [z, g, xbc, xbc, v, q, k, dt_raw]
    for arr in list(params) + list(consts):
        in_specs.append(_resident(arr.shape))
        args.append(arr)
    scratch = [pltpu.VMEM((SSD_N_GROUPS, SSD_D_STATE, SSD_HEADS_PER_GROUP * SSD_HEAD_DIM), F32),
               pltpu.VMEM((RET_N_HEADS, RET_QK_HEAD_DIM, RET_V_HEAD_DIM), F32),
               pltpu.VMEM((lb, LANES), F32),
               pltpu.VMEM((cps, LANES, CHUNK), F32),
               pltpu.VMEM((2 * lb, 2 * LANES), BF16)]
    return pl.pallas_call(
        functools.partial(_mixer_kernel, cps=cps),
        grid=(bsz, s // lb), in_specs=in_specs,
        out_specs=[_tok(lb, SSD_D_INNER), _tok(lb, RET_V_DIM)],
        out_shape=[jax.ShapeDtypeStruct((bsz, s, SSD_D_INNER), BF16),
                   jax.ShapeDtypeStruct((bsz, s, RET_V_DIM), BF16)],
        scratch_shapes=scratch, compiler_params=_params(), name="mixer",
    )(*args)


def _merge_kernel(ys_ref, yr_ref, gate_ref, h_ref, mod_ref, wbs_ref, wbr_ref, wo_ref, o_ref):
    d = o_ref.shape[-1]
    y_ssd = jnp.dot(ys_ref[...], wbs_ref[...], preferred_element_type=F32)
    y_ret = jnp.dot(yr_ref[...], wbr_ref[...], preferred_element_type=F32)
    mix = gate_ref[:, 0:d].astype(F32) * y_ssd + gate_ref[:, d:2 * d].astype(F32) * y_ret
    out = jnp.dot(mix.astype(BF16), wo_ref[...], preferred_element_type=F32)
    o_ref[...] = h_ref[...] + mod_ref[5:6, :] * out


def _merge_call(ys, yr, gates, h, mod, wbs, wbr, wo, *, tm):
    bsz, s, d = h.shape
    in_specs = [_tok(tm, ys.shape[-1]), _tok(tm, yr.shape[-1]), _tok(tm, gates.shape[-1]),
                _tok(tm, d), pl.BlockSpec((None, N_MOD, d), lambda b, i: (b, 0, 0)),
                _resident(wbs.shape), _resident(wbr.shape), _resident(wo.shape)]
    return pl.pallas_call(
        _merge_kernel, grid=(bsz, s // tm), in_specs=in_specs, out_specs=_tok(tm, d),
        out_shape=jax.ShapeDtypeStruct((bsz, s, d), F32),
        compiler_params=_params(), name="merge",
    )(ys, yr, gates, h, mod, wbs, wbr, wo)


def _retention_constants():
    L = CHUNK
    log_gamma = jnp.log1p(-jnp.exp2(-5.0 - jnp.arange(RET_N_HEADS, dtype=F32)))
    idx = jnp.arange(L, dtype=F32)
    rel = idx[:, None] - idx[None, :]
    dmask = jnp.where(rel >= 0, jnp.exp(log_gamma[:, None, None] * jnp.maximum(rel, 0.0)), 0.0)
    q_decay = jnp.exp(log_gamma[None, :] * (idx + 1.0)[:, None])
    k_decay = jnp.exp(log_gamma[None, :] * (L - 1.0 - idx)[:, None])
    qdec = jnp.repeat(q_decay, RET_QK_HEAD_DIM, axis=1)
    kdec = jnp.repeat(k_decay, RET_QK_HEAD_DIM, axis=1)
    sdec = jnp.broadcast_to(jnp.exp(log_gamma * L)[:, None, None], (RET_N_HEADS, 1, RET_V_HEAD_DIM))
    half = RET_QK_HEAD_DIM // 2
    inv_freq = ROPE_BASE ** (-jnp.arange(half, dtype=F32) / half)
    invf = jnp.concatenate([inv_freq, inv_freq]).reshape(1, RET_QK_HEAD_DIM)
    return invf, qdec, kdec, dmask, sdec


def _pad_lanes(v):
    return jnp.pad(v.reshape(1, -1), ((0, 0), (0, LANES - v.shape[-1])))


def kernel(x, c, positions, ada_w, ada_b, norm_ffn1_w, ffn1_w_in, ffn1_w_out, norm_mix_w, mix_w_in, mix_gate_b, ssd_conv_w, ssd_conv_b, ssd_dt_bias, ssd_a_log, ssd_d, ssd_norm_w, ret_norm_w, w_br_ssd, w_br_ret, mix_w_out, norm_ffn2_w, ffn2_w_in, ffn2_w_out, norm_final_w):
    bsz, s, d = x.shape
    assert s % CHUNK == 0 and d == RET_QK_DIM and ada_w.shape[0] == 1
    tm_ffn = min(512, s)
    tm_merge = min(512, s)
    tm_proj = min(1024, s)
    cps = min(4, s // CHUNK)
    invf, qdec, kdec, dmask, sdec = _retention_constants()
    pos = positions.reshape(bsz, s // LANES, LANES)
    l = 0

    mod = _ada_call(c, ada_w[l], ada_b[l]).reshape(bsz, N_MOD, d)

    w = mix_w_in[l].astype(BF16)
    o_z, o_xbc, o_dt = 0, 2048, 5120
    o_q, o_k, o_v, o_g, o_gates = 5152, 6176, 7200, 9248, 11296
    w_z = w[:, o_z:o_z + 2048]
    w_g = w[:, o_g:o_g + 2048]
    w_xbc = w[:, o_xbc:o_xbc + 3072]
    w_v = w[:, o_v:o_v + 2048]
    w_qk = w[:, o_q:o_q + 2048]
    w_gates = w[:, o_gates:o_gates + 2048]
    w_dt = jnp.pad(w[:, o_dt:o_dt + SSD_N_HEADS], ((0, 0), (0, LANES - SSD_N_HEADS)))

    h, u, dt_raw = _ffn_call(x, mod, norm_ffn1_w[l], ffn1_w_in[l].astype(BF16),
                             ffn1_w_out[l].astype(BF16), norm_mix_w[l], w_dt,
                             mode="mid", rows=(0, 1, 2), tm=tm_ffn)

    xbc, v = _proj_call(
        u, (("conv", w_xbc, (ssd_conv_w[l], ssd_conv_b[l].reshape(1, -1))), ("none", w_v, ())),
        tm=tm_proj, cw=512, name="proj_xbc_v")
    q, k, z = _proj_call(
        u, (("rotary", w_qk, (pos, invf)), ("silu", w_z, ())),
        tm=tm_proj, cw=512, name="proj_qk_z")
    g, gates = _proj_call(
        u, (("silu", w_g, ()), ("sigmoid_bias", w_gates, (mix_gate_b[l].reshape(1, -1),))),
        tm=tm_proj, cw=512, name="proj_g_gates")

    params = (_pad_lanes(ssd_dt_bias[l]), _pad_lanes(ssd_a_log[l]),
              jnp.repeat(ssd_d[l], SSD_HEAD_DIM).reshape(1, -1),
              ssd_norm_w[l].reshape(1, -1), ret_norm_w[l].reshape(1, -1))
    head_of_lane = jnp.arange(SSD_D_INNER) // SSD_HEAD_DIM
    expand = (jnp.arange(LANES)[:, None] == head_of_lane[None, :]).astype(BF16)
    expand = jnp.concatenate([expand, expand], axis=0)
    ys, yr = _mixer_call(z, g, xbc, v, q, k, dt_raw, params,
                         (dmask, sdec, expand, qdec.astype(BF16), kdec.astype(BF16)), cps=cps)

    h = _merge_call(ys, yr, gates, h, mod, w_br_ssd[l].astype(BF16), w_br_ret[l].astype(BF16),
                    mix_w_out[l].astype(BF16), tm=tm_merge)

    return _ffn_call(h, mod, norm_ffn2_w[l], ffn2_w_in[l].astype(BF16),
                     ffn2_w_out[l].astype(BF16), norm_final_w, None,
                     mode="final", rows=(6, 7, 8), tm=tm_ffn)
```

```python
import functools

import jax
import jax.numpy as jnp
from jax import lax
from jax.experimental import pallas as pl
from jax.experimental.pallas import tpu as pltpu

EPS = 1e-6
LOG2E = 1.4426950408889634
CHUNK = 128
SSD_HEAD_DIM = 64
SSD_N_HEADS = 32
SSD_N_GROUPS = 4
SSD_HEADS_PER_GROUP = 8
SSD_D_STATE = 128
SSD_D_INNER = 2048
SSD_CONV_WIDTH = 4
SSD_BC_DIM = 2 * SSD_N_GROUPS * SSD_D_STATE
RET_N_HEADS = 8
RET_QK_HEAD_DIM = 128
RET_V_HEAD_DIM = 256
RET_QK_DIM = 1024
RET_V_DIM = 2048
ROPE_BASE = 10000.0
FFN_RESIDUAL_WEIGHT = 0.5
N_MOD = 9
FFN_SUB_ROWS = 512

LANES = 128
SUBLANES = 8
VMEM_LIMIT_BYTES = 56 * 1024 * 1024

BF16 = jnp.bfloat16
F32 = jnp.float32


def _sigmoid(x):
    return 0.5 * jnp.tanh(0.5 * x) + 0.5


def _silu(x):
    hx = 0.5 * x
    return hx * (1.0 + jnp.tanh(hx))


def _rms_normalize(x):
    return x * lax.rsqrt(jnp.mean(x * x, axis=-1, keepdims=True) + EPS)


def _resident(shape):
    nd = len(shape)
    return pl.BlockSpec(shape, lambda *_: (0,) * nd, pipeline_mode=pl.Buffered(1))


def _tok(rows, width, col_block=0):
    return pl.BlockSpec((None, rows, width), lambda b, i: (b, i, col_block))


def _params(n_grid_axes=2):
    return pltpu.CompilerParams(dimension_semantics=("arbitrary",) * n_grid_axes,
                                vmem_limit_bytes=VMEM_LIMIT_BYTES)


def _ada_kernel(c_ref, w_ref, b_ref, o_ref):
    c_act = _silu(c_ref[...])
    o_ref[...] = jnp.dot(c_act.astype(BF16), w_ref[...].astype(BF16),
                         preferred_element_type=F32) + b_ref[...]


def _ada_call(c, w, b):
    bsz, d = c.shape
    n = w.shape[1]
    tn = d
    return pl.pallas_call(
        _ada_kernel,
        grid=(n // tn,),
        in_specs=[pl.BlockSpec((bsz, d), lambda j: (0, 0)),
                  pl.BlockSpec((d, tn), lambda j: (0, j)),
                  pl.BlockSpec((1, tn), lambda j: (0, j))],
        out_specs=pl.BlockSpec((bsz, tn), lambda j: (0, j)),
        out_shape=jax.ShapeDtypeStruct((bsz, n), F32),
        compiler_params=_params(1),
        name="ada_mod",
    )(c, w, b.reshape(1, n))


def _ffn_kernel(*refs, mode, rows, ffn_dim):
    if mode == "mid":
        (x_ref, mod_ref, nw_ref, win_ref, wout_ref, nw2_ref, wdt_ref,
         h_ref, u_ref, dt_ref) = refs
    else:
        x_ref, mod_ref, nw_ref, win_ref, wout_ref, nw2_ref, o_ref = refs
    sh_row, sc_row, g_row = rows
    mod = mod_ref[...]
    tm = x_ref.shape[0]
    sub = min(FFN_SUB_ROWS, tm)
    for r0 in range(0, tm, sub):
        sl = slice(r0, r0 + sub)
        h = x_ref[sl, :]
        u = _rms_normalize(h) * nw_ref[...] * (1.0 + mod[sc_row:sc_row + 1]) + mod[sh_row:sh_row + 1]
        gu = jnp.dot(u.astype(BF16), win_ref[...], preferred_element_type=F32)
        act = _silu(gu[:, :ffn_dim]) * gu[:, ffn_dim:]
        y = jnp.dot(act.astype(BF16), wout_ref[...], preferred_element_type=F32)
        h_new = h + FFN_RESIDUAL_WEIGHT * mod[g_row:g_row + 1] * y
        if mode == "mid":
            h_ref[sl, :] = h_new
            u2 = (_rms_normalize(h_new) * nw2_ref[...] * (1.0 + mod[sc_row + 3:sc_row + 4])
                  + mod[sh_row + 3:sh_row + 4]).astype(BF16)
            u_ref[sl, :] = u2
            dt_ref[sl, :] = jnp.dot(u2, wdt_ref[...], preferred_element_type=F32)
        else:
            o_ref[sl, :] = _rms_normalize(h_new) * nw2_ref[...]


def _ffn_call(x, mod, nw, w_in, w_out, nw2, w_dt, *, mode, rows, tm):
    bsz, s, d = x.shape
    ffn_dim = w_out.shape[0]
    in_specs = [_tok(tm, d), pl.BlockSpec((None, N_MOD, d), lambda b, i: (b, 0, 0)),
                _resident((1, d)), _resident(w_in.shape), _resident(w_out.shape), _resident((1, d))]
    args = [x, mod, nw.reshape(1, d), w_in, w_out, nw2.reshape(1, d)]
    if mode == "mid":
        in_specs.append(_resident(w_dt.shape))
        args.append(w_dt)
        out_specs = [_tok(tm, d), _tok(tm, d), _tok(tm, LANES)]
        out_shape = [jax.ShapeDtypeStruct((bsz, s, d), F32),
                     jax.ShapeDtypeStruct((bsz, s, d), BF16),
                     jax.ShapeDtypeStruct((bsz, s, LANES), F32)]
    else:
        out_specs = _tok(tm, d)
        out_shape = jax.ShapeDtypeStruct((bsz, s, d), F32)
    return pl.pallas_call(
        functools.partial(_ffn_kernel, mode=mode, rows=rows, ffn_dim=ffn_dim),
        grid=(bsz, s // tm), in_specs=in_specs, out_specs=out_specs, out_shape=out_shape,
        compiler_params=_params(), name="ffn_" + mode,
    )(*args)


def _parked_dot(u, w_chunk, buf_ref):
    rows = pl.ds(pl.multiple_of(jnp.minimum(pl.program_id(1), 0), SUBLANES), u.shape[0])
    buf_ref[rows, :] = jnp.dot(u, w_chunk, preferred_element_type=F32)
    return buf_ref[rows, :]


def _seg_act(product, col0, n_cols, cw, act, b_ref, o_ref):
    for c0 in range(0, n_cols, cw):
        r = product(col0 + c0, cw)
        if act == "silu":
            r = _silu(r)
        elif act == "sigmoid_bias":
            r = _sigmoid(r + b_ref[:, c0:c0 + cw])
        o_ref[:, c0:c0 + cw] = r.astype(o_ref.dtype)


def _seg_conv(product, col0, n_cols, cw, cw_ref, cb_ref, o_ref, tail_ref):
    tm = o_ref.shape[0]
    assert SSD_CONV_WIDTH == 4

    @pl.when(pl.program_id(1) == 0)
    def _():
        tail_ref[...] = jnp.zeros(tail_ref.shape, F32)

    sub = lax.broadcasted_iota(jnp.int32, (1, SUBLANES, 1), 1)

    def shift(x3, first, j):
        n = x3.shape[0]
        rot = pltpu.roll(jnp.concatenate([first, x3], axis=0), j, 1)
        return jnp.where(sub < j, rot[0:n], rot[1:n + 1])

    for c0 in range(0, n_cols, cw):
        r = product(col0 + c0, cw)
        r3 = r.reshape(tm // SUBLANES, SUBLANES, cw)
        w0, w1, w2, w3 = (cw_ref[k:k + 1, c0:c0 + cw].reshape(1, 1, cw) for k in range(4))
        bias = cb_ref[:, c0:c0 + cw].reshape(1, 1, cw)
        prev = tail_ref[:, c0:c0 + cw].reshape(1, SUBLANES, cw)
        s1 = shift(r3, prev, 1)
        b_term = w1 * r3 + w0 * s1
        b_prev = w1 * prev + w0 * shift(prev, prev, 1)
        acc = bias + w3 * r3 + w2 * s1 + shift(b_term, b_prev, 2)
        o_ref[:, c0:c0 + cw] = _silu(acc).reshape(tm, cw).astype(o_ref.dtype)
        tail_ref[:, c0:c0 + cw] = r[tm - SUBLANES:tm]


def _seg_rotary(product, col0, cw, pos_ref, invf_ref, q_ref, k_ref):
    tm = q_ref.shape[0]
    dk = RET_QK_HEAD_DIM
    nb = tm // LANES
    assert tm % (2 * LANES) == 0 and LANES % nb == 0
    lane = lax.broadcasted_iota(jnp.int32, (1, dk), 1)
    first_half = lane < dk // 2
    pos_rows = pos_ref[...].astype(F32)
    pos_t = jnp.concatenate([pos_rows] * (LANES // nb), axis=0).T
    ang = jnp.concatenate(
        [jnp.where(first_half, pos_t[:, j:j + 1], pos_t[:, j + nb // 2:j + nb // 2 + 1])
         for j in range(nb // 2)], axis=0) * invf_ref[...]
    cos_p, sin_p = jnp.cos(ang), jnp.sin(ang)
    cos_q, sin_q = pltpu.roll(cos_p, dk // 2, 1), pltpu.roll(sin_p, dk // 2, 1)
    cos = jnp.concatenate([jnp.where(first_half, cos_p, cos_q),
                           jnp.where(first_half, cos_q, cos_p)], axis=0)
    sin_signed = jnp.concatenate([jnp.where(first_half, -sin_p, sin_q),
                                  jnp.where(first_half, -sin_q, sin_p)], axis=0)
    k_scale = dk ** -0.5
    pieces = ((0, cos, sin_signed, q_ref),
              (RET_QK_DIM, cos * k_scale, sin_signed * k_scale, k_ref))
    for base, cs, sn, o_ref in pieces:
        for c0 in range(0, RET_QK_DIM, cw):
            r = product(col0 + base + c0, cw)
            for hh in range(cw // dk):
                lo = c0 + hh * dk
                x = r[:, hh * dk:(hh + 1) * dk]
                xr = x * cs + pltpu.roll(x, dk // 2, 1) * sn
                o_ref[:, lo:lo + dk] = xr.astype(o_ref.dtype)


N_PARK = 3

_SEG_ARITY = {"none": (0, 1, 0), "silu": (0, 1, 0), "sigmoid_bias": (1, 1, 0),
              "conv": (2, 1, 1), "rotary": (2, 2, 0)}


def _proj_kernel(u_ref, w_ref, *refs, plan, cw):
    n_in = sum(_SEG_ARITY[kind][0] for kind, _ in plan)
    n_out = sum(_SEG_ARITY[kind][1] for kind, _ in plan)
    ins, outs, scratch = list(refs[:n_in]), list(refs[n_in:n_in + n_out]), list(refs[n_in + n_out:])
    bufs = scratch[-N_PARK:]
    u = u_ref[...]
    n_products = [0]

    def product(c0, width):
        buf = bufs[n_products[0] % N_PARK]
        n_products[0] += 1
        return _parked_dot(u, w_ref[:, c0:c0 + width], buf)

    col0 = 0
    for kind, n_cols in plan:
        a_in, a_out, a_scr = _SEG_ARITY[kind]
        seg_in, ins = ins[:a_in], ins[a_in:]
        seg_out, outs = outs[:a_out], outs[a_out:]
        seg_scr, scratch = scratch[:a_scr], scratch[a_scr:]
        if kind == "conv":
            _seg_conv(product, col0, n_cols, cw, *seg_in, *seg_out, *seg_scr)
        elif kind == "rotary":
            _seg_rotary(product, col0, cw, *seg_in, *seg_out)
        else:
            _seg_act(product, col0, n_cols, cw, kind, seg_in[0] if seg_in else None, seg_out[0])
        col0 += n_cols


def _proj_call(u, segments, *, tm, cw, name):
    bsz, s, d = u.shape
    w_all = jnp.concatenate([w for _, w, _ in segments], axis=1)
    in_specs = [_tok(tm, d), _resident(w_all.shape)]
    args = [u, w_all]
    out_shape, out_specs, scratch, plan = [], [], [], []
    for kind, w, extra in segments:
        n = w.shape[1]
        plan.append((kind, n))
        for j, arr in enumerate(extra):
            tiled = kind == "rotary" and j == 0
            in_specs.append(_tok(tm // LANES, LANES) if tiled else _resident(arr.shape))
            args.append(arr)
        out_w = n // 2 if kind == "rotary" else n
        n_out = _SEG_ARITY[kind][1]
        out_shape += [jax.ShapeDtypeStruct((bsz, s, out_w), BF16)] * n_out
        out_specs += [_tok(tm, out_w)] * n_out
        if kind == "conv":
            scratch.append(pltpu.VMEM((SUBLANES, n), F32))
    scratch += [pltpu.VMEM((tm, cw), F32)] * N_PARK
    return pl.pallas_call(
        functools.partial(_proj_kernel, plan=tuple(plan), cw=cw),
        grid=(bsz, s // tm), in_specs=in_specs, out_specs=out_specs, out_shape=out_shape,
        scratch_shapes=scratch, compiler_params=_params(), name=name,
    )(*args)


def _mixer_kernel(z_ref, g_ref, xs_ref, bc_ref, v_ref, q_ref, k_ref, dt_ref,
                  dtb_ref, alog_ref, dskip_ref, snw_ref, rnw_ref, dmask_ref, sdec_ref, expand_ref,
                  qdec_ref, kdec_ref,
                  ys_ref, yr_ref, sstate, rstate, acs_s, srct_s, wsplit_s, *, cps):
    L = CHUNK
    hp = SSD_HEAD_DIM
    gw = SSD_HEADS_PER_GROUP * hp
    n_gn = SSD_N_GROUPS * SSD_D_STATE
    dk, dv = RET_QK_HEAD_DIM, RET_V_HEAD_DIM

    @pl.when(pl.program_id(1) == 0)
    def _():
        sstate[...] = jnp.zeros(sstate.shape, F32)
        rstate[...] = jnp.zeros(rstate.shape, F32)

    row = lax.broadcasted_iota(jnp.int32, (L, L), 0)
    col = lax.broadcasted_iota(jnp.int32, (L, L), 1)
    causal = row >= col
    tri = jnp.where(causal, 1.0, 0.0).astype(BF16)
    lane = lax.broadcasted_iota(jnp.int32, (1, LANES), 1)
    lo_half = lane < hp

    dt_all = jax.nn.softplus(dt_ref[...] + dtb_ref[...])
    a_all = dt_all * (-jnp.exp(alog_ref[...]))
    log_dt_all = jnp.log(dt_all)
    a1 = a_all.astype(BF16)
    r1 = a_all - a1.astype(F32)
    a2 = r1.astype(BF16)
    a3 = (r1 - a2.astype(F32)).astype(BF16)
    for ci in range(cps):
        sl = slice(ci * L, (ci + 1) * L)
        a_cs = (jnp.dot(tri, a1[sl], preferred_element_type=F32)
                + jnp.dot(tri, a2[sl], preferred_element_type=F32)
                + jnp.dot(tri, a3[sl], preferred_element_type=F32))
        a_end = a_cs[L - 1:L, :]
        acs_s[sl, :] = LOG2E * a_cs
        srct_s[ci] = (LOG2E * (a_cs - log_dt_all[sl])).T
        wts = jnp.concatenate([jnp.exp(a_cs), dt_all[sl] * jnp.exp(a_end - a_cs)], axis=0)
        w_hi = wts.astype(BF16)
        w_lo = (wts - w_hi.astype(F32)).astype(BF16)
        wsplit_s[2 * ci * L:2 * (ci + 1) * L, :] = jnp.concatenate([w_hi, w_lo], axis=1)

    def chunk_body(ci, carry):
        rows = pl.ds(pl.multiple_of(ci * L, L), L)
        a_cs2 = acs_s[rows, :]
        src_t2 = srct_s[ci]
        w_split = wsplit_s[pl.ds(pl.multiple_of(2 * ci * L, 2 * L), 2 * L), :]

        for g in range(SSD_N_GROUPS):
            gc = slice(g * gw, (g + 1) * gw)
            wide = jnp.dot(w_split, expand_ref[:, gc], preferred_element_type=F32)
            w_off = wide[0:L]
            w_state = wide[L:2 * L]
            b_g = bc_ref[rows, g * SSD_D_STATE:(g + 1) * SSD_D_STATE]
            c_g = bc_ref[rows, n_gn + g * SSD_D_STATE:n_gn + (g + 1) * SSD_D_STATE]
            cb = lax.dot_general(c_g, b_g, (((1,), (1,)), ((), ())), preferred_element_type=F32)
            st_t = sstate[g]
            y_off = jnp.dot(c_g, st_t.astype(BF16), preferred_element_type=F32) * w_off
            x_g = xs_ref[rows, gc]
            x_gf = x_g.astype(F32)
            xw = (x_gf * w_state).astype(BF16)
            sstate[g] = st_t * w_off[L - 1:L, :] + lax.dot_general(
                b_g, xw, (((0,), (0,)), ((), ())), preferred_element_type=F32)
            yg_parts = []
            for pr in range(SSD_HEADS_PER_GROUP // 2):
                h0 = g * SSD_HEADS_PER_GROUP + 2 * pr
                pc = slice(2 * pr * hp, (2 * pr + 2) * hp)
                scores = []
                for h in (h0, h0 + 1):
                    seg2 = a_cs2[:, h:h + 1] - src_t2[h:h + 1, :]
                    scores.append((cb * jnp.exp2(jnp.where(causal, seg2, -jnp.inf))).astype(BF16))
                x_pair = x_g[:, pc]
                zeros = jnp.zeros_like(x_pair)
                x_stack = jnp.concatenate([jnp.where(lo_half, x_pair, zeros),
                                           jnp.where(lo_half, zeros, x_pair)], axis=0)
                y_diag = jnp.dot(jnp.concatenate(scores, axis=1), x_stack,
                                 preferred_element_type=F32)
                c0 = g * gw + 2 * pr * hp
                y_pair = y_diag + y_off[:, pc] + x_gf[:, pc] * dskip_ref[:, c0:c0 + 2 * hp]
                yg_parts.append(y_pair * z_ref[rows, c0:c0 + 2 * hp].astype(F32))
            yg = jnp.concatenate(yg_parts, axis=1)
            ys_ref[rows, gc] = (_rms_normalize(yg) * snw_ref[:, gc]).astype(ys_ref.dtype)

        for h in range(RET_N_HEADS):
            hc = slice(h * dk, (h + 1) * dk)
            vh = v_ref[rows, h * dv:(h + 1) * dv]
            qh, kh = q_ref[rows, hc], k_ref[rows, hc]
            scores = lax.dot_general(qh, kh, (((1,), (1,)), ((), ())),
                                     preferred_element_type=F32) * dmask_ref[h]
            st = rstate[h]
            y = (jnp.dot(scores.astype(BF16), vh, preferred_element_type=F32)
                 + jnp.dot(qh * qdec_ref[:, hc], st.astype(BF16), preferred_element_type=F32))
            rstate[h] = st * sdec_ref[h] + lax.dot_general(
                kh * kdec_ref[:, hc], vh, (((0,), (0,)), ((), ())), preferred_element_type=F32)
            yn = _rms_normalize(y) * rnw_ref[:, h * dv:(h + 1) * dv]
            yr_ref[rows, h * dv:(h + 1) * dv] = (
                yn * g_ref[rows, h * dv:(h + 1) * dv].astype(F32)).astype(yr_ref.dtype)
        return carry

    lax.fori_loop(0, cps, chunk_body, 0, unroll=True)


def _mixer_call(z, g, xbc, v, q, k, dt_raw, params, consts, *, cps):
    bsz, s, _ = v.shape
    lb = cps * CHUNK
    in_specs = [_tok(lb, SSD_D_INNER), _tok(lb, RET_V_DIM),
                _tok(lb, SSD_D_INNER, 0),
                _tok(lb, SSD_BC_DIM, SSD_D_INNER // SSD_BC_DIM),
                _tok(lb, RET_V_DIM), _tok(lb, RET_QK_DIM), _tok(lb, RET_QK_DIM), _tok(lb, LANES)]
    args = [z, g, xbc, xbc, v, q, k, dt_raw]
    for arr in list(params) + list(consts):
        in_specs.append(_resident(arr.shape))
        args.append(arr)
    scratch = [pltpu.VMEM((SSD_N_GROUPS, SSD_D_STATE, SSD_HEADS_PER_GROUP * SSD_HEAD_DIM), F32),
               pltpu.VMEM((RET_N_HEADS, RET_QK_HEAD_DIM, RET_V_HEAD_DIM), F32),
               pltpu.VMEM((lb, LANES), F32),
               pltpu.VMEM((cps, LANES, CHUNK), F32),
               pltpu.VMEM((2 * lb, 2 * LANES), BF16)]
    return pl.pallas_call(
        functools.partial(_mixer_kernel, cps=cps),
        grid=(bsz, s // lb), in_specs=in_specs,
        out_specs=[_tok(lb, SSD_D_INNER), _tok(lb, RET_V_DIM)],
        out_shape=[jax.ShapeDtypeStruct((bsz, s, SSD_D_INNER), BF16),
                   jax.ShapeDtypeStruct((bsz, s, RET_V_DIM), BF16)],
        scratch_shapes=scratch, compiler_params=_params(), name="mixer",
    )(*args)


def _merge_kernel(ys_ref, yr_ref, gate_ref, h_ref, mod_ref, wbs_ref, wbr_ref, wo_ref, o_ref):
    d = o_ref.shape[-1]
    y_ssd = jnp.dot(ys_ref[...], wbs_ref[...], preferred_element_type=F32)
    y_ret = jnp.dot(yr_ref[...], wbr_ref[...], preferred_element_type=F32)
    mix = gate_ref[:, 0:d].astype(F32) * y_ssd + gate_ref[:, d:2 * d].astype(F32) * y_ret
    out = jnp.dot(mix.astype(BF16), wo_ref[...], preferred_element_type=F32)
    o_ref[...] = h_ref[...] + mod_ref[5:6, :] * out


def _merge_call(ys, yr, gates, h, mod, wbs, wbr, wo, *, tm):
    bsz, s, d = h.shape
    in_specs = [_tok(tm, ys.shape[-1]), _tok(tm, yr.shape[-1]), _tok(tm, gates.shape[-1]),
                _tok(tm, d), pl.BlockSpec((None, N_MOD, d), lambda b, i: (b, 0, 0)),
                _resident(wbs.shape), _resident(wbr.shape), _resident(wo.shape)]
    return pl.pallas_call(
        _merge_kernel, grid=(bsz, s // tm), in_specs=in_specs, out_specs=_tok(tm, d),
        out_shape=jax.ShapeDtypeStruct((bsz, s, d), F32),
        compiler_params=_params(), name="merge",
    )(ys, yr, gates, h, mod, wbs, wbr, wo)


def _retention_constants():
    L = CHUNK
    log_gamma = jnp.log1p(-jnp.exp2(-5.0 - jnp.arange(RET_N_HEADS, dtype=F32)))
    idx = jnp.arange(L, dtype=F32)
    rel = idx[:, None] - idx[None, :]
    dmask = jnp.where(rel >= 0, jnp.exp(log_gamma[:, None, None] * jnp.maximum(rel, 0.0)), 0.0)
    q_decay = jnp.exp(log_gamma[None, :] * (idx + 1.0)[:, None])
    k_decay = jnp.exp(log_gamma[None, :] * (L - 1.0 - idx)[:, None])
    qdec = jnp.repeat(q_decay, RET_QK_HEAD_DIM, axis=1)
    kdec = jnp.repeat(k_decay, RET_QK_HEAD_DIM, axis=1)
    sdec = jnp.broadcast_to(jnp.exp(log_gamma * L)[:, None, None], (RET_N_HEADS, 1, RET_V_HEAD_DIM))
    half = RET_QK_HEAD_DIM // 2
    inv_freq = ROPE_BASE ** (-jnp.arange(half, dtype=F32) / half)
    invf = jnp.concatenate([inv_freq, inv_freq]).reshape(1, RET_QK_HEAD_DIM)
    return invf, qdec, kdec, dmask, sdec


def _pad_lanes(v):
    return jnp.pad(v.reshape(1, -1), ((0, 0), (0, LANES - v.shape[-1])))


def kernel(x, c, positions, ada_w, ada_b, norm_ffn1_w, ffn1_w_in, ffn1_w_out, norm_mix_w, mix_w_in, mix_gate_b, ssd_conv_w, ssd_conv_b, ssd_dt_bias, ssd_a_log, ssd_d, ssd_norm_w, ret_norm_w, w_br_ssd, w_br_ret, mix_w_out, norm_ffn2_w, ffn2_w_in, ffn2_w_out, norm_final_w):
    bsz, s, d = x.shape
    assert s % CHUNK == 0 and d == RET_QK_DIM and ada_w.shape[0] == 1
    tm_ffn = min(1024, s)
    tm_merge = min(512, s)
    tm_proj = min(1024, s)
    cps = min(4, s // CHUNK)
    invf, qdec, kdec, dmask, sdec = _retention_constants()
    pos = positions.reshape(bsz, s // LANES, LANES)
    l = 0

    mod = _ada_call(c, ada_w[l], ada_b[l]).reshape(bsz, N_MOD, d)

    w = mix_w_in[l].astype(BF16)
    o_z, o_xbc, o_dt = 0, 2048, 5120
    o_q, o_k, o_v, o_g, o_gates = 5152, 6176, 7200, 9248, 11296
    w_z = w[:, o_z:o_z + 2048]
    w_g = w[:, o_g:o_g + 2048]
    w_xbc = w[:, o_xbc:o_xbc + 3072]
    w_v = w[:, o_v:o_v + 2048]
    w_qk = w[:, o_q:o_q + 2048]
    w_gates = w[:, o_gates:o_gates + 2048]
    w_dt = jnp.pad(w[:, o_dt:o_dt + SSD_N_HEADS], ((0, 0), (0, LANES - SSD_N_HEADS)))

    h, u, dt_raw = _ffn_call(x, mod, norm_ffn1_w[l], ffn1_w_in[l].astype(BF16),
                             ffn1_w_out[l].astype(BF16), norm_mix_w[l], w_dt,
                             mode="mid", rows=(0, 1, 2), tm=tm_ffn)

    xbc, v = _proj_call(
        u, (("conv", w_xbc, (ssd_conv_w[l], ssd_conv_b[l].reshape(1, -1))), ("none", w_v, ())),
        tm=tm_proj, cw=512, name="proj_xbc_v")
    q, k, z = _proj_call(
        u, (("rotary", w_qk, (pos, invf)), ("silu", w_z, ())),
        tm=tm_proj, cw=512, name="proj_qk_z")
    g, gates = _proj_call(
        u, (("silu", w_g, ()), ("sigmoid_bias", w_gates, (mix_gate_b[l].reshape(1, -1),))),
        tm=tm_proj, cw=512, name="proj_g_gates")

    params = (_pad_lanes(ssd_dt_bias[l]), _pad_lanes(ssd_a_log[l]),
              jnp.repeat(ssd_d[l], SSD_HEAD_DIM).reshape(1, -1),
              ssd_norm_w[l].reshape(1, -1), ret_norm_w[l].reshape(1, -1))
    head_of_lane = jnp.arange(SSD_D_INNER) // SSD_HEAD_DIM
    expand = (jnp.arange(LANES)[:, None] == head_of_lane[None, :]).astype(BF16)
    expand = jnp.concatenate([expand, expand], axis=0)
    ys, yr = _mixer_call(z, g, xbc, v, q, k, dt_raw, params,
                         (dmask, sdec, expand, qdec.astype(BF16), kdec.astype(BF16)), cps=cps)

    h = _merge_call(ys, yr, gates, h, mod, w_br_ssd[l].astype(BF16), w_br_ret[l].astype(BF16),
                    mix_w_out[l].astype(BF16), tm=tm_merge)

    return _ffn_call(h, mod, norm_ffn2_w[l], ffn2_w_in[l].astype(BF16),
                     ffn2_w_out[l].astype(BF16), norm_final_w, None,
                     mode="final", rows=(6, 7, 8), tm=tm_ffn)
```

```python
import functools

import jax
import jax.numpy as jnp
from jax import lax
from jax.experimental import pallas as pl
from jax.experimental.pallas import tpu as pltpu

EPS = 1e-6
LOG2E = 1.4426950408889634
CHUNK = 128
SSD_HEAD_DIM = 64
SSD_N_HEADS = 32
SSD_N_GROUPS = 4
SSD_HEADS_PER_GROUP = 8
SSD_D_STATE = 128
SSD_D_INNER = 2048
SSD_CONV_WIDTH = 4
SSD_BC_DIM = 2 * SSD_N_GROUPS * SSD_D_STATE
RET_N_HEADS = 8
RET_QK_HEAD_DIM = 128
RET_V_HEAD_DIM = 256
RET_QK_DIM = 1024
RET_V_DIM = 2048
ROPE_BASE = 10000.0
FFN_RESIDUAL_WEIGHT = 0.5
N_MOD = 9
FFN_SUB_ROWS = 512

LANES = 128
SUBLANES = 8
VMEM_LIMIT_BYTES = 56 * 1024 * 1024

BF16 = jnp.bfloat16
F32 = jnp.float32


def _sigmoid(x):
    return 0.5 * jnp.tanh(0.5 * x) + 0.5


def _silu(x):
    hx = 0.5 * x
    return hx * (1.0 + jnp.tanh(hx))


def _rms_normalize(x):
    return x * lax.rsqrt(jnp.mean(x * x, axis=-1, keepdims=True) + EPS)


def _resident(shape):
    nd = len(shape)
    return pl.BlockSpec(shape, lambda *_: (0,) * nd, pipeline_mode=pl.Buffered(1))


def _tok(rows, width, col_block=0):
    return pl.BlockSpec((None, rows, width), lambda b, i: (b, i, col_block))


def _params(n_grid_axes=2):
    return pltpu.CompilerParams(dimension_semantics=("arbitrary",) * n_grid_axes,
                                vmem_limit_bytes=VMEM_LIMIT_BYTES)


def _ada_kernel(c_ref, w_ref, b_ref, o_ref):
    c_act = _silu(c_ref[...])
    o_ref[...] = jnp.dot(c_act.astype(BF16), w_ref[...].astype(BF16),
                         preferred_element_type=F32) + b_ref[...]


def _ada_call(c, w, b):
    bsz, d = c.shape
    n = w.shape[1]
    tn = d
    return pl.pallas_call(
        _ada_kernel,
        grid=(n // tn,),
        in_specs=[pl.BlockSpec((bsz, d), lambda j: (0, 0)),
                  pl.BlockSpec((d, tn), lambda j: (0, j)),
                  pl.BlockSpec((1, tn), lambda j: (0, j))],
        out_specs=pl.BlockSpec((bsz, tn), lambda j: (0, j)),
        out_shape=jax.ShapeDtypeStruct((bsz, n), F32),
        compiler_params=_params(1),
        name="ada_mod",
    )(c, w, b.reshape(1, n))


def _ffn_kernel(*refs, mode, rows, ffn_dim):
    if mode == "mid":
        (x_ref, mod_ref, nw_ref, win_ref, wout_ref, nw2_ref, wdt_ref,
         h_ref, u_ref, dt_ref) = refs
    else:
        x_ref, mod_ref, nw_ref, win_ref, wout_ref, nw2_ref, o_ref = refs
    sh_row, sc_row, g_row = rows
    mod = mod_ref[...]
    tm = x_ref.shape[0]
    sub = min(FFN_SUB_ROWS, tm)
    for r0 in range(0, tm, sub):
        sl = slice(r0, r0 + sub)
        h = x_ref[sl, :]
        u = _rms_normalize(h) * nw_ref[...] * (1.0 + mod[sc_row:sc_row + 1]) + mod[sh_row:sh_row + 1]
        gu = jnp.dot(u.astype(BF16), win_ref[...], preferred_element_type=F32)
        act = _silu(gu[:, :ffn_dim]) * gu[:, ffn_dim:]
        y = jnp.dot(act.astype(BF16), wout_ref[...], preferred_element_type=F32)
        h_new = h + FFN_RESIDUAL_WEIGHT * mod[g_row:g_row + 1] * y
        if mode == "mid":
            h_ref[sl, :] = h_new
            u2 = (_rms_normalize(h_new) * nw2_ref[...] * (1.0 + mod[sc_row + 3:sc_row + 4])
                  + mod[sh_row + 3:sh_row + 4]).astype(BF16)
            u_ref[sl, :] = u2
            dt_ref[sl, :] = jnp.dot(u2, wdt_ref[...], preferred_element_type=F32)
        else:
            o_ref[sl, :] = _rms_normalize(h_new) * nw2_ref[...]


def _ffn_call(x, mod, nw, w_in, w_out, nw2, w_dt, *, mode, rows, tm):
    bsz, s, d = x.shape
    ffn_dim = w_out.shape[0]
    in_specs = [_tok(tm, d), pl.BlockSpec((None, N_MOD, d), lambda b, i: (b, 0, 0)),
                _resident((1, d)), _resident(w_in.shape), _resident(w_out.shape), _resident((1, d))]
    args = [x, mod, nw.reshape(1, d), w_in, w_out, nw2.reshape(1, d)]
    if mode == "mid":
        in_specs.append(_resident(w_dt.shape))
        args.append(w_dt)
        out_specs = [_tok(tm, d), _tok(tm, d), _tok(tm, LANES)]
        out_shape = [jax.ShapeDtypeStruct((bsz, s, d), F32),
                     jax.ShapeDtypeStruct((bsz, s, d), BF16),
                     jax.ShapeDtypeStruct((bsz, s, LANES), F32)]
    else:
        out_specs = _tok(tm, d)
        out_shape = jax.ShapeDtypeStruct((bsz, s, d), F32)
    return pl.pallas_call(
        functools.partial(_ffn_kernel, mode=mode, rows=rows, ffn_dim=ffn_dim),
        grid=(bsz, s // tm), in_specs=in_specs, out_specs=out_specs, out_shape=out_shape,
        compiler_params=_params(), name="ffn_" + mode,
    )(*args)


def _parked_dot(u, w_chunk, buf_ref):
    rows = pl.ds(pl.multiple_of(jnp.minimum(pl.program_id(1), 0), SUBLANES), u.shape[0])
    buf_ref[rows, :] = jnp.dot(u, w_chunk, preferred_element_type=F32)
    return buf_ref[rows, :]


def _seg_act(product, col0, n_cols, cw, act, b_ref, o_ref):
    for c0 in range(0, n_cols, cw):
        r = product(col0 + c0, cw)
        if act == "silu":
            r = _silu(r)
        elif act == "sigmoid_bias":
            r = _sigmoid(r + b_ref[:, c0:c0 + cw])
        o_ref[:, c0:c0 + cw] = r.astype(o_ref.dtype)


def _seg_conv(product, col0, n_cols, cw, cw_ref, cb_ref, o_ref, tail_ref):
    tm = o_ref.shape[0]
    assert SSD_CONV_WIDTH == 4

    @pl.when(pl.program_id(1) == 0)
    def _():
        tail_ref[...] = jnp.zeros(tail_ref.shape, F32)

    sub = lax.broadcasted_iota(jnp.int32, (1, SUBLANES, 1), 1)

    def shift(x3, first, j):
        n = x3.shape[0]
        rot = pltpu.roll(jnp.concatenate([first, x3], axis=0), j, 1)
        return jnp.where(sub < j, rot[0:n], rot[1:n + 1])

    for c0 in range(0, n_cols, cw):
        r = product(col0 + c0, cw)
        r3 = r.reshape(tm // SUBLANES, SUBLANES, cw)
        w0, w1, w2, w3 = (cw_ref[k:k + 1, c0:c0 + cw].reshape(1, 1, cw) for k in range(4))
        bias = cb_ref[:, c0:c0 + cw].reshape(1, 1, cw)
        prev = tail_ref[:, c0:c0 + cw].reshape(1, SUBLANES, cw)
        s1 = shift(r3, prev, 1)
        b_term = w1 * r3 + w0 * s1
        b_prev = w1 * prev + w0 * shift(prev, prev, 1)
        acc = bias + w3 * r3 + w2 * s1 + shift(b_term, b_prev, 2)
        o_ref[:, c0:c0 + cw] = _silu(acc).reshape(tm, cw).astype(o_ref.dtype)
        tail_ref[:, c0:c0 + cw] = r[tm - SUBLANES:tm]


def _seg_rotary(product, col0, cw, pos_ref, invf_ref, q_ref, k_ref):
    tm = q_ref.shape[0]
    dk = RET_QK_HEAD_DIM
    nb = tm // LANES
    assert tm % (2 * LANES) == 0 and LANES % nb == 0
    lane = lax.broadcasted_iota(jnp.int32, (1, dk), 1)
    first_half = lane < dk // 2
    pos_rows = pos_ref[...].astype(F32)
    pos_t = jnp.concatenate([pos_rows] * (LANES // nb), axis=0).T
    ang = jnp.concatenate(
        [jnp.where(first_half, pos_t[:, j:j + 1], pos_t[:, j + nb // 2:j + nb // 2 + 1])
         for j in range(nb // 2)], axis=0) * invf_ref[...]
    cos_p, sin_p = jnp.cos(ang), jnp.sin(ang)
    cos_q, sin_q = pltpu.roll(cos_p, dk // 2, 1), pltpu.roll(sin_p, dk // 2, 1)
    cos = jnp.concatenate([jnp.where(first_half, cos_p, cos_q),
                           jnp.where(first_half, cos_q, cos_p)], axis=0)
    sin_signed = jnp.concatenate([jnp.where(first_half, -sin_p, sin_q),
                                  jnp.where(first_half, -sin_q, sin_p)], axis=0)
    k_scale = dk ** -0.5
    pieces = ((0, cos, sin_signed, q_ref),
              (RET_QK_DIM, cos * k_scale, sin_signed * k_scale, k_ref))
    for base, cs, sn, o_ref in pieces:
        for c0 in range(0, RET_QK_DIM, cw):
            r = product(col0 + base + c0, cw)
            for hh in range(cw // dk):
                lo = c0 + hh * dk
                x = r[:, hh * dk:(hh + 1) * dk]
                xr = x * cs + pltpu.roll(x, dk // 2, 1) * sn
                o_ref[:, lo:lo + dk] = xr.astype(o_ref.dtype)


N_PARK = 3

_SEG_ARITY = {"none": (0, 1, 0), "silu": (0, 1, 0), "sigmoid_bias": (1, 1, 0),
              "conv": (2, 1, 1), "rotary": (2, 2, 0)}


def _proj_kernel(u_ref, w_ref, *refs, plan, cw):
    n_in = sum(_SEG_ARITY[kind][0] for kind, _ in plan)
    n_out = sum(_SEG_ARITY[kind][1] for kind, _ in plan)
    ins, outs, scratch = list(refs[:n_in]), list(refs[n_in:n_in + n_out]), list(refs[n_in + n_out:])
    bufs = scratch[-N_PARK:]
    u = u_ref[...]
    n_products = [0]

    def product(c0, width):
        buf = bufs[n_products[0] % N_PARK]
        n_products[0] += 1
        return _parked_dot(u, w_ref[:, c0:c0 + width], buf)

    col0 = 0
    for kind, n_cols in plan:
        a_in, a_out, a_scr = _SEG_ARITY[kind]
        seg_in, ins = ins[:a_in], ins[a_in:]
        seg_out, outs = outs[:a_out], outs[a_out:]
        seg_scr, scratch = scratch[:a_scr], scratch[a_scr:]
        if kind == "conv":
            _seg_conv(product, col0, n_cols, cw, *seg_in, *seg_out, *seg_scr)
        elif kind == "rotary":
            _seg_rotary(product, col0, cw, *seg_in, *seg_out)
        else:
            _seg_act(product, col0, n_cols, cw, kind, seg_in[0] if seg_in else None, seg_out[0])
        col0 += n_cols


def _proj_call(u, segments, *, tm, cw, name):
    bsz, s, d = u.shape
    w_all = jnp.concatenate([w for _, w, _ in segments], axis=1)
    in_specs = [_tok(tm, d), _resident(w_all.shape)]
    args = [u, w_all]
    out_shape, out_specs, scratch, plan = [], [], [], []
    for kind, w, extra in segments:
        n = w.shape[1]
        plan.append((kind, n))
        for j, arr in enumerate(extra):
            tiled = kind == "rotary" and j == 0
            in_specs.append(_tok(tm // LANES, LANES) if tiled else _resident(arr.shape))
            args.append(arr)
        out_w = n // 2 if kind == "rotary" else n
        n_out = _SEG_ARITY[kind][1]
        out_shape += [jax.ShapeDtypeStruct((bsz, s, out_w), BF16)] * n_out
        out_specs += [_tok(tm, out_w)] * n_out
        if kind == "conv":
            scratch.append(pltpu.VMEM((SUBLANES, n), F32))
    scratch += [pltpu.VMEM((tm, cw), F32)] * N_PARK
    return pl.pallas_call(
        functools.partial(_proj_kernel, plan=tuple(plan), cw=cw),
        grid=(bsz, s // tm), in_specs=in_specs, out_specs=out_specs, out_shape=out_shape,
        scratch_shapes=scratch, compiler_params=_params(), name=name,
    )(*args)


def _mixer_kernel(z_ref, g_ref, xs_ref, bc_ref, v_ref, q_ref, k_ref, dt_ref,
                  dtb_ref, alog_ref, dskip_ref, dmask_ref, sdec_ref, expand_ref,
                  qdec_ref, kdec_ref,
                  ys_ref, yr_ref, sstate, rstate, acs_s, srct_s, wsplit_s, *, cps):
    L = CHUNK
    hp = SSD_HEAD_DIM
    gw = SSD_HEADS_PER_GROUP * hp
    n_gn = SSD_N_GROUPS * SSD_D_STATE
    dk, dv = RET_QK_HEAD_DIM, RET_V_HEAD_DIM

    @pl.when(pl.program_id(1) == 0)
    def _():
        sstate[...] = jnp.zeros(sstate.shape, F32)
        rstate[...] = jnp.zeros(rstate.shape, F32)

    row = lax.broadcasted_iota(jnp.int32, (L, L), 0)
    col = lax.broadcasted_iota(jnp.int32, (L, L), 1)
    causal = row >= col
    tri = jnp.where(causal, 1.0, 0.0).astype(BF16)
    lane = lax.broadcasted_iota(jnp.int32, (1, LANES), 1)
    lo_half = lane < hp

    dt_all = jax.nn.softplus(dt_ref[...] + dtb_ref[...])
    a_all = dt_all * (-jnp.exp(alog_ref[...]))
    log_dt_all = jnp.log(dt_all)
    a1 = a_all.astype(BF16)
    r1 = a_all - a1.astype(F32)
    a2 = r1.astype(BF16)
    a3 = (r1 - a2.astype(F32)).astype(BF16)
    for ci in range(cps):
        sl = slice(ci * L, (ci + 1) * L)
        a_cs = (jnp.dot(tri, a1[sl], preferred_element_type=F32)
                + jnp.dot(tri, a2[sl], preferred_element_type=F32)
                + jnp.dot(tri, a3[sl], preferred_element_type=F32))
        a_end = a_cs[L - 1:L, :]
        acs_s[sl, :] = LOG2E * a_cs
        srct_s[ci] = (LOG2E * (a_cs - log_dt_all[sl])).T
        wts = jnp.concatenate([jnp.exp(a_cs), dt_all[sl] * jnp.exp(a_end - a_cs)], axis=0)
        w_hi = wts.astype(BF16)
        w_lo = (wts - w_hi.astype(F32)).astype(BF16)
        wsplit_s[2 * ci * L:2 * (ci + 1) * L, :] = jnp.concatenate([w_hi, w_lo], axis=1)

    def chunk_body(ci, carry):
        rows = pl.ds(pl.multiple_of(ci * L, L), L)
        a_cs2 = acs_s[rows, :]
        src_t2 = srct_s[ci]
        w_split = wsplit_s[pl.ds(pl.multiple_of(2 * ci * L, 2 * L), 2 * L), :]

        for g in range(SSD_N_GROUPS):
            gc = slice(g * gw, (g + 1) * gw)
            wide = jnp.dot(w_split, expand_ref[:, gc], preferred_element_type=F32)
            w_off = wide[0:L]
            w_state = wide[L:2 * L]
            b_g = bc_ref[rows, g * SSD_D_STATE:(g + 1) * SSD_D_STATE]
            c_g = bc_ref[rows, n_gn + g * SSD_D_STATE:n_gn + (g + 1) * SSD_D_STATE]
            cb = lax.dot_general(c_g, b_g, (((1,), (1,)), ((), ())), preferred_element_type=F32)
            st_t = sstate[g]
            y_off = jnp.dot(c_g, st_t.astype(BF16), preferred_element_type=F32) * w_off
            x_g = xs_ref[rows, gc]
            x_gf = x_g.astype(F32)
            xw = (x_gf * w_state).astype(BF16)
            sstate[g] = st_t * w_off[L - 1:L, :] + lax.dot_general(
                b_g, xw, (((0,), (0,)), ((), ())), preferred_element_type=F32)
            yg_parts = []
            for pr in range(SSD_HEADS_PER_GROUP // 2):
                h0 = g * SSD_HEADS_PER_GROUP + 2 * pr
                pc = slice(2 * pr * hp, (2 * pr + 2) * hp)
                scores = []
                for h in (h0, h0 + 1):
                    seg2 = a_cs2[:, h:h + 1] - src_t2[h:h + 1, :]
                    scores.append((cb * jnp.exp2(jnp.where(causal, seg2, -jnp.inf))).astype(BF16))
                x_pair = x_g[:, pc]
                zeros = jnp.zeros_like(x_pair)
                x_stack = jnp.concatenate([jnp.where(lo_half, x_pair, zeros),
                                           jnp.where(lo_half, zeros, x_pair)], axis=0)
                y_diag = jnp.dot(jnp.concatenate(scores, axis=1), x_stack,
                                 preferred_element_type=F32)
                c0 = g * gw + 2 * pr * hp
                y_pair = y_diag + y_off[:, pc] + x_gf[:, pc] * dskip_ref[:, c0:c0 + 2 * hp]
                yg_parts.append(y_pair * z_ref[rows, c0:c0 + 2 * hp].astype(F32))
            yg = jnp.concatenate(yg_parts, axis=1)
            ys_ref[rows, gc] = _rms_normalize(yg).astype(ys_ref.dtype)

        for h in range(RET_N_HEADS):
            hc = slice(h * dk, (h + 1) * dk)
            vh = v_ref[rows, h * dv:(h + 1) * dv]
            qh, kh = q_ref[rows, hc], k_ref[rows, hc]
            scores = lax.dot_general(qh, kh, (((1,), (1,)), ((), ())),
                                     preferred_element_type=F32) * dmask_ref[h]
            st = rstate[h]
            y = (jnp.dot(scores.astype(BF16), vh, preferred_element_type=F32)
                 + jnp.dot(qh * qdec_ref[:, hc], st.astype(BF16), preferred_element_type=F32))
            rstate[h] = st * sdec_ref[h] + lax.dot_general(
                kh * kdec_ref[:, hc], vh, (((0,), (0,)), ((), ())), preferred_element_type=F32)
            yn = _rms_normalize(y)
            yr_ref[rows, h * dv:(h + 1) * dv] = (
                yn * g_ref[rows, h * dv:(h + 1) * dv].astype(F32)).astype(yr_ref.dtype)
        return carry

    lax.fori_loop(0, cps, chunk_body, 0, unroll=True)


def _mixer_call(z, g, xbc, v, q, k, dt_raw, params, consts, *, cps):
    bsz, s, _ = v.shape
    lb = cps * CHUNK
    in_specs = [_tok(lb, SSD_D_INNER), _tok(lb, RET_V_DIM),
                _tok(lb, SSD_D_INNER, 0),
                _tok(lb, SSD_BC_DIM, SSD_D_INNER // SSD_BC_DIM),
                _tok(lb, RET_V_DIM), _tok(lb, RET_QK_DIM), _tok(lb, RET_QK_DIM), _tok(lb, LANES)]
    args = [z, g, xbc, xbc, v, q, k, dt_raw]
    for arr in list(params) + list(consts):
        in_specs.append(_resident(arr.shape))
        args.append(arr)
    scratch = [pltpu.VMEM((SSD_N_GROUPS, SSD_D_STATE, SSD_HEADS_PER_GROUP * SSD_HEAD_DIM), F32),
               pltpu.VMEM((RET_N_HEADS, RET_QK_HEAD_DIM, RET_V_HEAD_DIM), F32),
               pltpu.VMEM((lb, LANES), F32),
               pltpu.VMEM((cps, LANES, CHUNK), F32),
               pltpu.VMEM((2 * lb, 2 * LANES), BF16)]
    return pl.pallas_call(
        functools.partial(_mixer_kernel, cps=cps),
        grid=(bsz, s // lb), in_specs=in_specs,
        out_specs=[_tok(lb, SSD_D_INNER), _tok(lb, RET_V_DIM)],
        out_shape=[jax.ShapeDtypeStruct((bsz, s, SSD_D_INNER), BF16),
                   jax.ShapeDtypeStruct((bsz, s, RET_V_DIM), BF16)],
        scratch_shapes=scratch, compiler_params=_params(), name="mixer",
    )(*args)


def _merge_kernel(ys_ref, yr_ref, gate_ref, h_ref, mod_ref, wbs_ref, wbr_ref, wo_ref, o_ref):
    d = o_ref.shape[-1]
    y_ssd = jnp.dot(ys_ref[...], wbs_ref[...], preferred_element_type=F32)
    y_ret = jnp.dot(yr_ref[...], wbr_ref[...], preferred_element_type=F32)
    mix = gate_ref[:, 0:d].astype(F32) * y_ssd + gate_ref[:, d:2 * d].astype(F32) * y_ret
    out = jnp.dot(mix.astype(BF16), wo_ref[...], preferred_element_type=F32)
    o_ref[...] = h_ref[...] + mod_ref[5:6, :] * out


def _merge_call(ys, yr, gates, h, mod, wbs, wbr, wo, *, tm):
    bsz, s, d = h.shape
    in_specs = [_tok(tm, ys.shape[-1]), _tok(tm, yr.shape[-1]), _tok(tm, gates.shape[-1]),
                _tok(tm, d), pl.BlockSpec((None, N_MOD, d), lambda b, i: (b, 0, 0)),
                _resident(wbs.shape), _resident(wbr.shape), _resident(wo.shape)]
    return pl.pallas_call(
        _merge_kernel, grid=(bsz, s // tm), in_specs=in_specs, out_specs=_tok(tm, d),
        out_shape=jax.ShapeDtypeStruct((bsz, s, d), F32),
        compiler_params=_params(), name="merge",
    )(ys, yr, gates, h, mod, wbs, wbr, wo)


def _retention_constants():
    L = CHUNK
    log_gamma = jnp.log1p(-jnp.exp2(-5.0 - jnp.arange(RET_N_HEADS, dtype=F32)))
    idx = jnp.arange(L, dtype=F32)
    rel = idx[:, None] - idx[None, :]
    dmask = jnp.where(rel >= 0, jnp.exp(log_gamma[:, None, None] * jnp.maximum(rel, 0.0)), 0.0)
    q_decay = jnp.exp(log_gamma[None, :] * (idx + 1.0)[:, None])
    k_decay = jnp.exp(log_gamma[None, :] * (L - 1.0 - idx)[:, None])
    qdec = jnp.repeat(q_decay, RET_QK_HEAD_DIM, axis=1)
    kdec = jnp.repeat(k_decay, RET_QK_HEAD_DIM, axis=1)
    sdec = jnp.broadcast_to(jnp.exp(log_gamma * L)[:, None, None], (RET_N_HEADS, 1, RET_V_HEAD_DIM))
    half = RET_QK_HEAD_DIM // 2
    inv_freq = ROPE_BASE ** (-jnp.arange(half, dtype=F32) / half)
    invf = jnp.concatenate([inv_freq, inv_freq]).reshape(1, RET_QK_HEAD_DIM)
    return invf, qdec, kdec, dmask, sdec


def _pad_lanes(v):
    return jnp.pad(v.reshape(1, -1), ((0, 0), (0, LANES - v.shape[-1])))


def kernel(x, c, positions, ada_w, ada_b, norm_ffn1_w, ffn1_w_in, ffn1_w_out, norm_mix_w, mix_w_in, mix_gate_b, ssd_conv_w, ssd_conv_b, ssd_dt_bias, ssd_a_log, ssd_d, ssd_norm_w, ret_norm_w, w_br_ssd, w_br_ret, mix_w_out, norm_ffn2_w, ffn2_w_in, ffn2_w_out, norm_final_w):
    bsz, s, d = x.shape
    assert s % CHUNK == 0 and d == RET_QK_DIM and ada_w.shape[0] == 1
    tm_ffn = min(1024, s)
    tm_merge = min(512, s)
    tm_proj = min(1024, s)
    cps = min(4, s // CHUNK)
    invf, qdec, kdec, dmask, sdec = _retention_constants()
    pos = positions.reshape(bsz, s // LANES, LANES)
    l = 0

    mod = _ada_call(c, ada_w[l], ada_b[l]).reshape(bsz, N_MOD, d)

    w = mix_w_in[l].astype(BF16)
    o_z, o_xbc, o_dt = 0, 2048, 5120
    o_q, o_k, o_v, o_g, o_gates = 5152, 6176, 7200, 9248, 11296
    w_z = w[:, o_z:o_z + 2048]
    w_g = w[:, o_g:o_g + 2048]
    w_xbc = w[:, o_xbc:o_xbc + 3072]
    w_v = w[:, o_v:o_v + 2048]
    w_qk = w[:, o_q:o_q + 2048]
    w_gates = w[:, o_gates:o_gates + 2048]
    w_dt = jnp.pad(w[:, o_dt:o_dt + SSD_N_HEADS], ((0, 0), (0, LANES - SSD_N_HEADS)))

    h, u, dt_raw = _ffn_call(x, mod, norm_ffn1_w[l], ffn1_w_in[l].astype(BF16),
                             ffn1_w_out[l].astype(BF16), norm_mix_w[l], w_dt,
                             mode="mid", rows=(0, 1, 2), tm=tm_ffn)

    xbc, v = _proj_call(
        u, (("conv", w_xbc, (ssd_conv_w[l], ssd_conv_b[l].reshape(1, -1))), ("none", w_v, ())),
        tm=tm_proj, cw=512, name="proj_xbc_v")
    q, k, z = _proj_call(
        u, (("rotary", w_qk, (pos, invf)), ("silu", w_z, ())),
        tm=tm_proj, cw=512, name="proj_qk_z")
    g, gates = _proj_call(
        u, (("silu", w_g, ()), ("sigmoid_bias", w_gates, (mix_gate_b[l].reshape(1, -1),))),
        tm=tm_proj, cw=512, name="proj_g_gates")

    params = (_pad_lanes(ssd_dt_bias[l]), _pad_lanes(ssd_a_log[l]),
              jnp.repeat(ssd_d[l], SSD_HEAD_DIM).reshape(1, -1))
    head_of_lane = jnp.arange(SSD_D_INNER) // SSD_HEAD_DIM
    expand = (jnp.arange(LANES)[:, None] == head_of_lane[None, :]).astype(BF16)
    expand = jnp.concatenate([expand, expand], axis=0)
    ys, yr = _mixer_call(z, g, xbc, v, q, k, dt_raw, params,
                         (dmask, sdec, expand, qdec.astype(BF16), kdec.astype(BF16)), cps=cps)

    wbs = (ssd_norm_w[l][:, None] * w_br_ssd[l]).astype(BF16)
    wbr = (ret_norm_w[l][:, None] * w_br_ret[l]).astype(BF16)
    h = _merge_call(ys, yr, gates, h, mod, wbs, wbr, mix_w_out[l].astype(BF16), tm=tm_merge)

    return _ffn_call(h, mod, norm_ffn2_w[l], ffn2_w_in[l].astype(BF16),
                     ffn2_w_out[l].astype(BF16), norm_final_w, None,
                     mode="final", rows=(6, 7, 8), tm=tm_ffn)
```
